```python
import jax, jax.numpy as jnp
from jax import lax
import numpy as np

D_MODEL = 1024
BATCH = 4
SEQ = 4096
DEPTH = 4
DEC_BATCH = 32
DEC_SEQ = 1
PAST_LEN = 8192
PAGE_SIZE = 128

N_HEADS = 8
HEAD_DIM = 64
ATT_WIDTH = N_HEADS * HEAD_DIM
POOL_WINDOWS = (2, 4, 8, 16)
POOL_GROUP = 128
POOL_WIDTH = POOL_GROUP * len(POOL_WINDOWS)
POOL_CTX = max(POOL_WINDOWS) - 1
FFN_HIDDEN = ((8 * D_MODEL // 3 + 255) // 256) * 256
PLE_DIM = 256
Q_BLOCK = 128
RMS_EPS = 1e-6
NEG_INF = -1e30
IN_WIDTH = 3 * ATT_WIDTH + N_HEADS + POOL_WIDTH + 2 * D_MODEL
_SPLITS = (ATT_WIDTH, 2 * ATT_WIDTH, 3 * ATT_WIDTH, 3 * ATT_WIDTH + N_HEADS,
           3 * ATT_WIDTH + N_HEADS + POOL_WIDTH,
           3 * ATT_WIDTH + N_HEADS + POOL_WIDTH + D_MODEL)

kernel_name = 'forgetting_pool_hybrid_step'


def _rmsnorm(x, g):
    xf = x.astype(jnp.float32)
    y = xf * lax.rsqrt(jnp.mean(xf * xf, axis=-1, keepdims=True) + RMS_EPS)
    return (y * g.astype(jnp.float32)).astype(x.dtype)


def _forget_attention(q, k, v, c_q, c_k, q_pos, k_pos):
    B, Sq, H, Dh = q.shape
    scale = HEAD_DIM ** -0.5
    kf = k.astype(jnp.float32)
    ck = jnp.transpose(c_k, (0, 2, 1))[:, :, None, :]

    def block(args):
        qb, cqb, pb = args
        s = jnp.einsum('bqhd,bkhd->bhqk', qb.astype(jnp.float32), kf) * scale
        s = s + jnp.transpose(cqb, (0, 2, 1))[:, :, :, None] - ck
        mask = k_pos[None, :] <= pb[:, None]
        s = jnp.where(mask[None, None], s, NEG_INF)
        w = jax.nn.softmax(s, axis=-1)
        return jnp.einsum('bhqk,bkhd->bqhd', w.astype(v.dtype), v)

    if Sq > Q_BLOCK and Sq % Q_BLOCK == 0:
        nb = Sq // Q_BLOCK
        qb = jnp.moveaxis(q.reshape(B, nb, Q_BLOCK, H, Dh), 1, 0)
        cqb = jnp.moveaxis(c_q.reshape(B, nb, Q_BLOCK, H), 1, 0)
        pb = q_pos.reshape(nb, Q_BLOCK)
        out = lax.map(block, (qb, cqb, pb))
        return jnp.moveaxis(out, 0, 1).reshape(B, Sq, H, Dh)
    return block((q, c_q, q_pos))


def _pool_mixer(u, n_ctx, w_grp, scale):
    B, L, C = u.shape
    uf = u.astype(jnp.float32)
    cs = jnp.concatenate([jnp.zeros((B, 1, C), jnp.float32), jnp.cumsum(uf, axis=1)], axis=1)
    t = jnp.arange(n_ctx, L)
    outs = []
    for g, w in enumerate(POOL_WINDOWS):
        lo = jnp.maximum(t + 1 - w, 0)
        sl = slice(g * POOL_GROUP, (g + 1) * POOL_GROUP)
        win_sum = cs[:, t + 1, sl] - cs[:, lo, sl]
        cnt = (t + 1 - lo).astype(jnp.float32)[None, :, None]
        outs.append(win_sum / cnt - uf[:, n_ctx:, sl])
    d = jnp.stack(outs, axis=2)
    y = jnp.einsum('btgc,gcd->btgd', d, w_grp.astype(jnp.float32)).reshape(B, L - n_ctx, C)
    return (y * scale.astype(jnp.float32)).astype(u.dtype)


def _layer(x, p, prm, past):
    (norm_mix, w_in, b_forget, q_gain, k_gain, w_pool_grp, pool_scale, w_up_attn,
     w_up_pool, w_out, norm_ffn, w_gate_up, w_down, norm_ple, w_ple_gate, w_ple_proj) = prm
    B, S, _ = x.shape
    h = _rmsnorm(x, norm_mix)
    q, k, v, f_logit, u, g_attn, g_pool = jnp.split(h @ w_in, _SPLITS, axis=-1)
    q = _rmsnorm(q.reshape(B, S, N_HEADS, HEAD_DIM), q_gain)
    k = _rmsnorm(k.reshape(B, S, N_HEADS, HEAD_DIM), k_gain)
    v = v.reshape(B, S, N_HEADS, HEAD_DIM)
    logf = jax.nn.log_sigmoid(f_logit.astype(jnp.float32) + b_forget.astype(jnp.float32))
    if past is None:
        k_all, v_all, logf_all, u_all, n_ctx, p0 = k, v, logf, u, 0, 0
    else:
        k_past, v_past, logf_past, pool_buf = past
        p0 = k_past.shape[1]
        k_all = jnp.concatenate([k_past, k], axis=1)
        v_all = jnp.concatenate([v_past, v], axis=1)
        logf_all = jnp.concatenate([logf_past.astype(jnp.float32), logf], axis=1)
        u_all = jnp.concatenate([pool_buf, u], axis=1)
        n_ctx = pool_buf.shape[1]
    c_all = jnp.cumsum(logf_all, axis=1)
    q_pos = p0 + jnp.arange(S)
    k_pos = jnp.arange(p0 + S)
    attn = _forget_attention(q, k_all, v_all, c_all[:, p0:], c_all, q_pos, k_pos).reshape(B, S, ATT_WIDTH)
    pool = _pool_mixer(u_all, n_ctx, w_pool_grp, pool_scale)
    mixed = jax.nn.sigmoid(g_attn) * (attn @ w_up_attn) + jax.nn.sigmoid(g_pool) * (pool @ w_up_pool)
    x = x + mixed @ w_out
    gate, up = jnp.split(_rmsnorm(x, norm_ffn) @ w_gate_up, 2, axis=-1)
    x = x + (jax.nn.silu(gate) * up) @ w_down
    x = x + jax.nn.sigmoid(_rmsnorm(x, norm_ple) @ w_ple_gate) * (p @ w_ple_proj)
    return x, (k, v, logf, u_all[:, -POOL_CTX:])


def setup_inputs(seed: int = 0) -> dict:
    key = jax.random.key(seed)
    ks = jax.random.split(key, 32)
    f32 = jnp.float32
    n_pages = PAST_LEN // PAGE_SIZE
    n_used = DEC_BATCH * n_pages
    n_pool = n_used + max(1, n_used // 4)

    def nrm(k, shape, scale=1.0):
        return jax.random.normal(k, shape, f32) * scale

    page_table = jax.random.permutation(ks[0], n_pool)[:n_used].reshape(DEC_BATCH, n_pages).astype(jnp.int32)
    return {
        'x_prompt': nrm(ks[1], (BATCH, SEQ, D_MODEL)),
        'x_sample': nrm(ks[2], (DEC_BATCH, DEC_SEQ, D_MODEL)),
        'cache_k': nrm(ks[3], (DEPTH, n_pool, PAGE_SIZE, N_HEADS, HEAD_DIM)),
        'cache_v': nrm(ks[4], (DEPTH, n_pool, PAGE_SIZE, N_HEADS, HEAD_DIM)),
        'cache_logf': jax.nn.log_sigmoid(nrm(ks[5], (DEPTH, n_pool, PAGE_SIZE, N_HEADS)) + 2.0),
        'state_pool': nrm(ks[6], (DEPTH, DEC_BATCH, POOL_CTX, POOL_WIDTH)),
        'page_table': page_table,
        'p_prompt': nrm(ks[7], (DEPTH, BATCH, SEQ, PLE_DIM)),
        'p_sample': nrm(ks[8], (DEPTH, DEC_BATCH, DEC_SEQ, PLE_DIM)),
        'norm_mix': 1.0 + nrm(ks[9], (DEPTH, D_MODEL), 0.05),
        'w_in': nrm(ks[10], (DEPTH, D_MODEL, IN_WIDTH), D_MODEL ** -0.5),
        'b_forget': 2.0 + nrm(ks[11], (DEPTH, N_HEADS), 0.1),
        'q_gain': 1.0 + nrm(ks[12], (DEPTH, HEAD_DIM), 0.05),
        'k_gain': 1.0 + nrm(ks[13], (DEPTH, HEAD_DIM), 0.05),
        'w_pool_grp': nrm(ks[14], (DEPTH, len(POOL_WINDOWS), POOL_GROUP, POOL_GROUP), POOL_GROUP ** -0.5),
        'pool_scale': 1.0 + nrm(ks[15], (DEPTH, POOL_WIDTH), 0.1),
        'w_up_attn': nrm(ks[16], (DEPTH, ATT_WIDTH, D_MODEL), ATT_WIDTH ** -0.5),
        'w_up_pool': nrm(ks[17], (DEPTH, POOL_WIDTH, D_MODEL), POOL_WIDTH ** -0.5),
        'w_out': nrm(ks[18], (DEPTH, D_MODEL, D_MODEL), D_MODEL ** -0.5),
        'norm_ffn': 1.0 + nrm(ks[19], (DEPTH, D_MODEL), 0.05),
        'w_gate_up': nrm(ks[20], (DEPTH, D_MODEL, 2 * FFN_HIDDEN), D_MODEL ** -0.5),
        'w_down': nrm(ks[21], (DEPTH, FFN_HIDDEN, D_MODEL), FFN_HIDDEN ** -0.5),
        'norm_ple': 1.0 + nrm(ks[22], (DEPTH, D_MODEL), 0.05),
        'w_ple_gate': nrm(ks[23], (DEPTH, D_MODEL, D_MODEL), D_MODEL ** -0.5),
        'w_ple_proj': nrm(ks[24], (DEPTH, PLE_DIM, D_MODEL), PLE_DIM ** -0.5),
    }


def reference(x_prompt, x_sample, cache_k, cache_v, cache_logf, state_pool, page_table,
              p_prompt, p_sample, norm_mix, w_in, b_forget, q_gain, k_gain, w_pool_grp,
              pool_scale, w_up_attn, w_up_pool, w_out, norm_ffn, w_gate_up, w_down,
              norm_ple, w_ple_gate, w_ple_proj):
    weights = (norm_mix, w_in, b_forget, q_gain, k_gain, w_pool_grp, pool_scale, w_up_attn,
               w_up_pool, w_out, norm_ffn, w_gate_up, w_down, norm_ple, w_ple_gate, w_ple_proj)
    n_seq = page_table.shape[0]
    xp, xs = x_prompt, x_sample
    prompt_states, sample_states = [], []
    for i in range(DEPTH):
        prm = tuple(w[i] for w in weights)
        xp, st_p = _layer(xp, p_prompt[i], prm, None)
        past = (cache_k[i][page_table].reshape(n_seq, -1, N_HEADS, HEAD_DIM),
                cache_v[i][page_table].reshape(n_seq, -1, N_HEADS, HEAD_DIM),
                cache_logf[i][page_table].reshape(n_seq, -1, N_HEADS),
                state_pool[i])
        xs, st_s = _layer(xs, p_sample[i], prm, past)
        prompt_states.append(st_p)
        sample_states.append(st_s)
    k_p, v_p, lf_p, pool_p = [jnp.stack(a) for a in zip(*prompt_states)]
    k_s, v_s, lf_s, pool_s = [jnp.stack(a) for a in zip(*sample_states)]
    return (xp, xs, k_p, v_p, lf_p, pool_p, k_s, v_s, lf_s, pool_s)
```

```python
import functools

import jax
import jax.numpy as jnp
from jax import lax
from jax.experimental import pallas as pl
from jax.experimental.pallas import tpu as pltpu

F32 = jnp.float32
BF16 = jnp.bfloat16

N_HEADS = 8
HEAD_DIM = 64
ATT_WIDTH = N_HEADS * HEAD_DIM
POOL_WINDOWS = (2, 4, 8, 16)
POOL_GROUP = 128
POOL_WIDTH = POOL_GROUP * len(POOL_WINDOWS)
POOL_CTX = max(POOL_WINDOWS) - 1
HALO = POOL_CTX + 1
RMS_EPS = 1e-6
NEG_INF = -1e30
LANES = 128
DEC_ROWS = 128
PAGES_PER_STEP = 8
FFN_CHUNKS = 2
VMEM_LIMIT = 52 * 1024 * 1024

_NT = (((1,), (1,)), ((), ()))


def _const_spec(block, index):
    return pl.BlockSpec(block, index, pipeline_mode=pl.Buffered(1))


def _rms_rows(x, gain):
    return x * lax.rsqrt(jnp.mean(x * x, axis=-1, keepdims=True) + RMS_EPS) * gain


def _log_sigmoid(x):
    return -(jnp.maximum(-x, 0.0) + jnp.log1p(jnp.exp(-jnp.abs(x))))


def _cumsum_lanes(x):
    n = x.shape[-1]
    lane = lax.broadcasted_iota(jnp.int32, x.shape, x.ndim - 1)
    shift = 1
    while shift < n:
        x = x + jnp.where(lane >= shift, pltpu.roll(x, shift, axis=x.ndim - 1), 0.0)
        shift *= 2
    return x


def _inproj_kernel(x_ref, nrm_ref, wqkv_ref, wf_ref, wu_ref, wg_ref, bf_ref, qg_ref, kg_ref,
                   qT_ref, kT_ref, vT_ref, lfT_ref, cT_ref, u_ref, g_ref, carry_ref):
    tm = x_ref.shape[0]
    h = _rms_rows(x_ref[...], nrm_ref[...]).astype(BF16)

    def proj_t(rows):
        return lax.dot_general(rows, h, _NT, preferred_element_type=F32)

    def head_norm(y_t, gain_col):
        parts = []
        for hd in range(N_HEADS):
            y = y_t[hd * HEAD_DIM:(hd + 1) * HEAD_DIM]
            parts.append(y * lax.rsqrt(jnp.mean(y * y, axis=0, keepdims=True) + RMS_EPS))
        return jnp.concatenate(parts, axis=0) * gain_col

    q_t = head_norm(proj_t(wqkv_ref[0:ATT_WIDTH, :]), qg_ref[...])
    qT_ref[...] = (q_t * (HEAD_DIM ** -0.5)).astype(BF16)
    kT_ref[...] = head_norm(proj_t(wqkv_ref[ATT_WIDTH:2 * ATT_WIDTH, :]), kg_ref[...])
    vT_ref[...] = proj_t(wqkv_ref[2 * ATT_WIDTH:3 * ATT_WIDTH, :])

    logf = _log_sigmoid(proj_t(wf_ref[...])[0:N_HEADS] + bf_ref[...])
    lfT_ref[...] = logf

    @pl.when(pl.program_id(1) == 0)
    def _():
        carry_ref[...] = jnp.zeros_like(carry_ref)

    c = _cumsum_lanes(logf) + carry_ref[:, 0:1]
    cT_ref[...] = c
    carry_ref[...] = jnp.broadcast_to(c[:, tm - 1:tm], carry_ref.shape)

    u_ref[...] = jnp.dot(h, wu_ref[...], preferred_element_type=F32)
    g_ref[...] = jnp.dot(h, wg_ref[...], preferred_element_type=F32)


def _inproj(x, w, layer, tm):
    b, s, d = x.shape
    lay = lambda *_: (layer, 0, 0)
    tok_t = lambda rows: pl.BlockSpec((None, rows, tm), lambda i, j: (i, 0, j))
    tok = lambda cols: pl.BlockSpec((None, tm, cols), lambda i, j: (i, j, 0))
    return pl.pallas_call(
        _inproj_kernel,
        grid=(b, s // tm),
        in_specs=[
            tok(d),
            _const_spec((None, 1, d), lay),
            _const_spec((None, 3 * ATT_WIDTH, d), lay),
            _const_spec((None, 16, d), lay),
            _const_spec((None, d, POOL_WIDTH), lay),
            _const_spec((None, d, 2 * d), lay),
            _const_spec((None, N_HEADS, 1), lay),
            _const_spec((None, ATT_WIDTH, 1), lay),
            _const_spec((None, ATT_WIDTH, 1), lay),
        ],
        out_specs=[tok_t(ATT_WIDTH), tok_t(ATT_WIDTH), tok_t(ATT_WIDTH), tok_t(N_HEADS), tok_t(N_HEADS),
                   tok(POOL_WIDTH), tok(2 * d)],
        out_shape=[
            jax.ShapeDtypeStruct((b, ATT_WIDTH, s), BF16),
            jax.ShapeDtypeStruct((b, ATT_WIDTH, s), F32),
            jax.ShapeDtypeStruct((b, ATT_WIDTH, s), F32),
            jax.ShapeDtypeStruct((b, N_HEADS, s), F32),
            jax.ShapeDtypeStruct((b, N_HEADS, s), F32),
            jax.ShapeDtypeStruct((b, s, POOL_WIDTH), F32),
            jax.ShapeDtypeStruct((b, s, 2 * d), F32),
        ],
        scratch_shapes=[pltpu.VMEM((N_HEADS, LANES), F32)],
        compiler_params=pltpu.CompilerParams(dimension_semantics=("parallel", "arbitrary"),
                                             vmem_limit_bytes=VMEM_LIMIT),
        name="inproj",
    )(x, w["norm_mix"], w["wqkv_t"], w["wf_t"], w["w_u"], w["w_g"], w["b_forget"], w["q_gain"], w["k_gain"])


def _attn_kernel(qi_tab, kj_tab, qT_ref, kT_ref, vT_ref, cT_ref, o_ref, qs_ref, p_ref, m_ref, l_ref, acc_ref):
    tq = o_ref.shape[0]
    tk = kT_ref.shape[1]
    pair = pl.program_id(1)
    t = pl.program_id(2)
    qi = qi_tab[t]
    kj = kj_tab[t]

    @pl.when(kj == 0)
    def _():
        q = qT_ref[...].astype(F32).T
        first = lax.broadcasted_iota(jnp.int32, q.shape, 1) < HEAD_DIM
        qs_ref[0:tq] = jnp.where(first, q, 0.0).astype(BF16)
        qs_ref[tq:2 * tq] = jnp.where(first, 0.0, q).astype(BF16)
        m_ref[...] = jnp.full_like(m_ref, NEG_INF)
        l_ref[...] = jnp.zeros_like(l_ref)
        acc_ref[...] = jnp.zeros_like(acc_ref)

    def step(on_diagonal):
        s = jnp.dot(qs_ref[...], kT_ref[...].astype(BF16), preferred_element_type=F32)
        alphas = []
        for half in range(2):
            rows = slice(half * tq, (half + 1) * tq)
            sh = s[rows] - cT_ref[pl.ds(2 * pair + half, 1), :]
            if on_diagonal:
                row = lax.broadcasted_iota(jnp.int32, (tq, tk), 0)
                col = lax.broadcasted_iota(jnp.int32, (tq, tk), 1)
                sh = jnp.where(col <= row, sh, NEG_INF)
            m_prev = m_ref[rows]
            m_new = jnp.maximum(m_prev, jnp.max(sh, axis=1, keepdims=True))
            alpha = jnp.exp(m_prev - m_new)
            p = jnp.exp(sh - m_new)
            l_ref[rows] = alpha * l_ref[rows] + jnp.sum(p, axis=1, keepdims=True)
            m_ref[rows] = m_new
            p_ref[rows] = p.astype(BF16)
            alphas.append(alpha)
        pv = lax.dot_general(p_ref[...], vT_ref[...].astype(BF16), _NT, preferred_element_type=F32)
        for half in range(2):
            rows = slice(half * tq, (half + 1) * tq)
            acc_ref[rows] = alphas[half] * acc_ref[rows] + pv[rows]

    @pl.when(kj < qi)
    def _():
        step(False)

    @pl.when(kj == qi)
    def _():
        step(True)
        first = lax.broadcasted_iota(jnp.int32, (tq, LANES), 1) < HEAD_DIM
        out = jnp.where(first, acc_ref[0:tq] / l_ref[0:tq], acc_ref[tq:2 * tq] / l_ref[tq:2 * tq])
        o_ref[...] = out.astype(o_ref.dtype)


def _prompt_attention(q_t, k_t, v_t, c_t, tile):
    b, _, s = q_t.shape
    n = s // tile
    qi_tab = jnp.asarray([i for i in range(n) for _ in range(i + 1)], jnp.int32)
    kj_tab = jnp.asarray([j for i in range(n) for j in range(i + 1)], jnp.int32)
    pairs = N_HEADS // 2
    grid_spec = pltpu.PrefetchScalarGridSpec(
        num_scalar_prefetch=2,
        grid=(b, pairs, n * (n + 1) // 2),
        in_specs=[
            pl.BlockSpec((None, LANES, tile), lambda i, p, t, qi, kj: (i, p, qi[t])),
            pl.BlockSpec((None, LANES, tile), lambda i, p, t, qi, kj: (i, p, kj[t])),
            pl.BlockSpec((None, LANES, tile), lambda i, p, t, qi, kj: (i, p, kj[t])),
            pl.BlockSpec((None, N_HEADS, tile), lambda i, p, t, qi, kj: (i, 0, kj[t])),
        ],
        out_specs=pl.BlockSpec((None, tile, LANES), lambda i, p, t, qi, kj: (i, qi[t], p)),
        scratch_shapes=[
            pltpu.VMEM((2 * tile, LANES), BF16),
            pltpu.VMEM((2 * tile, tile), BF16),
            pltpu.VMEM((2 * tile, 1), F32),
            pltpu.VMEM((2 * tile, 1), F32),
            pltpu.VMEM((2 * tile, LANES), F32),
        ],
    )
    return pl.pallas_call(
        _attn_kernel,
        grid_spec=grid_spec,
        out_shape=jax.ShapeDtypeStruct((b, s, ATT_WIDTH), BF16),
        compiler_params=pltpu.CompilerParams(dimension_semantics=("parallel", "parallel", "arbitrary"),
                                             vmem_limit_bytes=VMEM_LIMIT),
        name="prompt_attention",
    )(qi_tab, kj_tab, q_t, k_t, v_t, c_t)


def _decode_attn_kernel(pt_ref, *refs):
    g_pages = PAGES_PER_STEP
    k_refs = refs[0:g_pages]
    v_refs = refs[g_pages:2 * g_pages]
    lf_refs = refs[2 * g_pages:3 * g_pages]
    qT_ref, knT_ref, vnT_ref, lfnT_ref, o_ref, q_ref, kn_ref, vn_ref, lfn_ref, m_ref, l_ref, carry_ref, acc_ref = refs[3 * g_pages:]
    seq = pl.program_id(0)
    grp = pl.program_id(1)
    last = pl.num_programs(1) - 1

    def column(ref):
        x = ref[...].astype(F32)
        lane = lax.broadcasted_iota(jnp.int32, x.shape, 1)
        return jnp.sum(jnp.where(lane == seq, x, 0.0), axis=1, keepdims=True)

    @pl.when(jnp.logical_and(seq == 0, grp == 0))
    def _():
        o_ref[...] = jnp.zeros_like(o_ref)

    @pl.when(grp == 0)
    def _():
        q_ref[...] = jnp.broadcast_to(column(qT_ref), q_ref.shape)
        kn_ref[...] = column(knT_ref)
        vn_ref[...] = column(vnT_ref)
        lfn_ref[...] = column(lfnT_ref)
        m_ref[...] = jnp.full_like(m_ref, NEG_INF)
        l_ref[...] = jnp.zeros_like(l_ref)
        carry_ref[...] = jnp.zeros_like(carry_ref)
        acc_ref[...] = jnp.zeros_like(acc_ref)

    cs = []
    carry = carry_ref[...]
    for g in range(g_pages):
        c = _cumsum_lanes(lf_refs[g][...]) + carry
        cs.append(c)
        carry = c[:, LANES - 1:LANES]
    carry_ref[...] = carry

    for hd in range(N_HEADS):
        q_h = q_ref[hd * HEAD_DIM:(hd + 1) * HEAD_DIM]
        s = [jnp.sum(k_refs[g][hd] * q_h, axis=0, keepdims=True) - cs[g][hd:hd + 1] for g in range(g_pages)]
        m_prev = m_ref[hd:hd + 1]
        m_new = m_prev
        for g in range(g_pages):
            m_new = jnp.maximum(m_new, jnp.max(s[g], axis=1, keepdims=True))
        alpha = jnp.exp(m_prev - m_new)
        l_new = alpha * l_ref[hd:hd + 1]
        acc = alpha * acc_ref[hd]
        for g in range(g_pages):
            p = jnp.exp(s[g] - m_new)
            l_new = l_new + jnp.sum(p, axis=1, keepdims=True)
            acc = acc + p * v_refs[g][hd]
        acc_ref[hd] = acc
        l_ref[hd:hd + 1] = l_new
        m_ref[hd:hd + 1] = m_new

    @pl.when(grp == last)
    def _():
        outs = []
        c_new = carry_ref[...] + lfn_ref[...]
        for hd in range(N_HEADS):
            rows = slice(hd * HEAD_DIM, (hd + 1) * HEAD_DIM)
            s_new = jnp.sum(q_ref[rows] * kn_ref[rows], axis=0, keepdims=True) - c_new[hd:hd + 1]
            m_prev = m_ref[hd:hd + 1]
            m_fin = jnp.maximum(m_prev, s_new)
            alpha = jnp.exp(m_prev - m_fin)
            p_new = jnp.exp(s_new - m_fin)
            l_fin = alpha * l_ref[hd:hd + 1] + p_new
            past = jnp.sum(acc_ref[hd], axis=1, keepdims=True)
            outs.append((alpha * past + p_new * vn_ref[rows]) / l_fin)
        out_col = jnp.concatenate(outs, axis=0)
        lane = lax.broadcasted_iota(jnp.int32, o_ref.shape, 1)
        o_ref[...] = jnp.where(lane == seq, out_col, o_ref[...])


def _decode_attention(page_table, cache_k_t, cache_v_t, cache_lf_t, q_t, kn_t, vn_t, lfn_t, layer):
    n_seq, n_pages = page_table.shape
    g_pages = PAGES_PER_STEP
    pt = page_table.reshape(-1)

    def page(g):
        return lambda b, j, pt_ref: (layer, pt_ref[b * n_pages + j * g_pages + g], 0, 0, 0)

    def page_lf(g):
        return lambda b, j, pt_ref: (layer, pt_ref[b * n_pages + j * g_pages + g], 0, 0)

    kv_block = (None, None, N_HEADS, HEAD_DIM, LANES)
    whole = lambda rows: pl.BlockSpec((rows, LANES), lambda b, j, pt_ref: (0, 0))
    grid_spec = pltpu.PrefetchScalarGridSpec(
        num_scalar_prefetch=1,
        grid=(n_seq, n_pages // g_pages),
        in_specs=([pl.BlockSpec(kv_block, page(g)) for g in range(g_pages)]
                  + [pl.BlockSpec(kv_block, page(g)) for g in range(g_pages)]
                  + [pl.BlockSpec((None, None, N_HEADS, LANES), page_lf(g)) for g in range(g_pages)]
                  + [whole(ATT_WIDTH), whole(ATT_WIDTH), whole(ATT_WIDTH), whole(N_HEADS)]),
        out_specs=whole(ATT_WIDTH),
        scratch_shapes=[
            pltpu.VMEM((ATT_WIDTH, LANES), F32),
            pltpu.VMEM((ATT_WIDTH, 1), F32),
            pltpu.VMEM((ATT_WIDTH, 1), F32),
            pltpu.VMEM((N_HEADS, 1), F32),
            pltpu.VMEM((N_HEADS, 1), F32),
            pltpu.VMEM((N_HEADS, 1), F32),
            pltpu.VMEM((N_HEADS, 1), F32),
            pltpu.VMEM((N_HEADS, HEAD_DIM, LANES), F32),
        ],
    )
    return pl.pallas_call(
        _decode_attn_kernel,
        grid_spec=grid_spec,
        out_shape=jax.ShapeDtypeStruct((ATT_WIDTH, LANES), F32),
        compiler_params=pltpu.CompilerParams(dimension_semantics=("arbitrary", "arbitrary"),
                                             vmem_limit_bytes=VMEM_LIMIT),
        name="decode_attention",
    )(pt, *([cache_k_t] * g_pages), *([cache_v_t] * g_pages), *([cache_lf_t] * g_pages), q_t, kn_t, vn_t, lfn_t)


def _mix_tail(d_groups, attn, gates, x, wp_ref, ps_ref, wua_ref, wup_ref, wo_ref):
    pooled = []
    for g in range(len(POOL_WINDOWS)):
        y = jnp.dot(d_groups[g].astype(BF16), wp_ref[g], preferred_element_type=F32)
        pooled.append((y * ps_ref[:, g * POOL_GROUP:(g + 1) * POOL_GROUP]).astype(BF16))
    pool = jnp.concatenate(pooled, axis=1)
    d_model = x.shape[1]
    a = jnp.dot(attn, wua_ref[...], preferred_element_type=F32)
    b = jnp.dot(pool, wup_ref[...], preferred_element_type=F32)
    mixed = jax.nn.sigmoid(gates[:, 0:d_model]) * a + jax.nn.sigmoid(gates[:, d_model:2 * d_model]) * b
    return x + jnp.dot(mixed.astype(BF16), wo_ref[...], preferred_element_type=F32)


def _mix_prompt_kernel(attn_ref, u_ref, g_ref, x_ref, wp_ref, ps_ref, wua_ref, wup_ref, wo_ref, o_ref, ext_ref):
    tm = u_ref.shape[0]
    j = pl.program_id(1)

    @pl.when(j == 0)
    def _():
        ext_ref[0:HALO] = jnp.zeros((HALO, POOL_WIDTH), F32)

    @pl.when(j > 0)
    def _():
        ext_ref[0:HALO] = ext_ref[tm:tm + HALO]

    ext_ref[HALO:HALO + tm] = u_ref[...]
    pos = j * tm + lax.broadcasted_iota(jnp.int32, (tm, 1), 0)
    d_groups = []
    for g, w in enumerate(POOL_WINDOWS):
        lanes = slice(g * POOL_GROUP, (g + 1) * POOL_GROUP)
        win = ext_ref[HALO:HALO + tm, lanes]
        for back in range(1, w):
            win = win + ext_ref[HALO - back:HALO - back + tm, lanes]
        cnt = jnp.minimum(pos + 1, w).astype(F32)
        d_groups.append(win / cnt - ext_ref[HALO:HALO + tm, lanes])
    o_ref[...] = _mix_tail(d_groups, attn_ref[...], g_ref[...], x_ref[...], wp_ref, ps_ref, wua_ref, wup_ref, wo_ref)


def _mix_decode_kernel(attn_ref, u_ref, g_ref, x_ref, st_ref, wp_ref, ps_ref, wua_ref, wup_ref, wo_ref, o_ref):
    tm = u_ref.shape[0]
    n_seq = st_ref.shape[1]
    d_groups = []
    for g, w in enumerate(POOL_WINDOWS):
        lanes = slice(g * POOL_GROUP, (g + 1) * POOL_GROUP)
        u_new = u_ref[0:n_seq, lanes]
        win = u_new
        for back in range(1, w):
            win = win + st_ref[POOL_CTX - back, :, lanes]
        d = win / float(w) - u_new
        d_groups.append(jnp.concatenate([d, jnp.zeros((tm - n_seq, POOL_GROUP), F32)], axis=0))
    o_ref[...] = _mix_tail(d_groups, attn_ref[...], g_ref[...], x_ref[...], wp_ref, ps_ref, wua_ref, wup_ref, wo_ref)


def _mix_weight_specs(w, layer, d):
    lay3 = lambda *_: (layer, 0, 0)
    lay4 = lambda *_: (layer, 0, 0, 0)
    specs = [
        _const_spec((None, len(POOL_WINDOWS), POOL_GROUP, POOL_GROUP), lay4),
        _const_spec((None, 1, POOL_WIDTH), lay3),
        _const_spec((None, ATT_WIDTH, d), lay3),
        _const_spec((None, POOL_WIDTH, d), lay3),
        _const_spec((None, d, d), lay3),
    ]
    return specs, (w["w_pool_grp"], w["pool_scale"], w["w_up_attn"], w["w_up_pool"], w["w_out"])


def _mix_prompt(attn, u, g, x, w, layer, tm):
    b, s, d = x.shape
    tok = lambda cols: pl.BlockSpec((None, tm, cols), lambda i, j: (i, j, 0))
    w_specs, w_args = _mix_weight_specs(w, layer, d)
    return pl.pallas_call(
        _mix_prompt_kernel,
        grid=(b, s // tm),
        in_specs=[tok(ATT_WIDTH), tok(POOL_WIDTH), tok(2 * d), tok(d)] + w_specs,
        out_specs=tok(d),
        out_shape=jax.ShapeDtypeStruct((b, s, d), F32),
        scratch_shapes=[pltpu.VMEM((HALO + tm, POOL_WIDTH), F32)],
        compiler_params=pltpu.CompilerParams(dimension_semantics=("parallel", "arbitrary"),
                                             vmem_limit_bytes=VMEM_LIMIT),
        name="mix_prompt",
    )(attn, u, g, x, *w_args)


def _mix_decode(attn, u, g, x, state_t, w, layer):
    rows, d = x.shape
    n_seq = state_t.shape[2]
    full = lambda cols: pl.BlockSpec((rows, cols), lambda i: (0, 0))
    w_specs, w_args = _mix_weight_specs(w, layer, d)
    return pl.pallas_call(
        _mix_decode_kernel,
        grid=(1,),
        in_specs=[full(ATT_WIDTH), full(POOL_WIDTH), full(2 * d), full(d),
                  pl.BlockSpec((None, POOL_CTX, n_seq, POOL_WIDTH), lambda i: (layer, 0, 0, 0))] + w_specs,
        out_specs=full(d),
        out_shape=jax.ShapeDtypeStruct((rows, d), F32),
        compiler_params=pltpu.CompilerParams(dimension_semantics=("arbitrary",), vmem_limit_bytes=VMEM_LIMIT),
        name="mix_decode",
    )(attn, u, g, x, state_t, *w_args)


def _ffn_ple_kernel(x_ref, p_ref, nf_ref, wgate_ref, wup_ref, wdown_ref, np_ref, wpg_ref, wpp_ref, o_ref):
    x = x_ref[...]
    h = _rms_rows(x, nf_ref[...]).astype(BF16)
    hidden = wgate_ref.shape[1]
    chunk = hidden // FFN_CHUNKS
    y = x
    for c in range(FFN_CHUNKS):
        cols = slice(c * chunk, (c + 1) * chunk)
        gate = jnp.dot(h, wgate_ref[:, cols], preferred_element_type=F32)
        up = jnp.dot(h, wup_ref[:, cols], preferred_element_type=F32)
        act = (jax.nn.silu(gate) * up).astype(BF16)
        y = y + jnp.dot(act, wdown_ref[cols, :], preferred_element_type=F32)
    h2 = _rms_rows(y, np_ref[...]).astype(BF16)
    gate2 = jax.nn.sigmoid(jnp.dot(h2, wpg_ref[...], preferred_element_type=F32))
    emb = jnp.dot(p_ref[...].astype(BF16), wpp_ref[...], preferred_element_type=F32)
    o_ref[...] = y + gate2 * emb


def _ffn_ple(x, p_all, w, layer, tm):
    m, d = x.shape
    ple = p_all.shape[2]
    hidden = w["w_down"].shape[1]
    lay = lambda *_: (layer, 0, 0)
    return pl.pallas_call(
        _ffn_ple_kernel,
        grid=(m // tm,),
        in_specs=[
            pl.BlockSpec((tm, d), lambda i: (i, 0)),
            pl.BlockSpec((None, tm, ple), lambda i: (layer, i, 0)),
            _const_spec((None, 1, d), lay),
            _const_spec((None, d, hidden), lay),
            _const_spec((None, d, hidden), lambda *_: (layer, 0, 1)),
            _const_spec((None, hidden, d), lay),
            _const_spec((None, 1, d), lay),
            _const_spec((None, d, d), lay),
            _const_spec((None, ple, d), lay),
        ],
        out_specs=pl.BlockSpec((tm, d), lambda i: (i, 0)),
        out_shape=jax.ShapeDtypeStruct((m, d), F32),
        compiler_params=pltpu.CompilerParams(dimension_semantics=("parallel",), vmem_limit_bytes=VMEM_LIMIT),
        name="ffn_ple",
    )(x, p_all, w["norm_ffn"], w["w_gate_up"], w["w_gate_up"], w["w_down"], w["norm_ple"], w["w_ple_gate"], w["w_ple_proj"])


def _prepare_weights(norm_mix, w_in, b_forget, q_gain, k_gain, w_pool_grp, pool_scale, w_up_attn, w_up_pool,
                     w_out, norm_ffn, w_gate_up, w_down, norm_ple, w_ple_gate, w_ple_proj):
    a = ATT_WIDTH
    f0, u0, g0 = 3 * a, 3 * a + N_HEADS, 3 * a + N_HEADS + POOL_WIDTH
    w_in_t = jnp.swapaxes(w_in, 1, 2)
    wf_t = jnp.pad(w_in_t[:, f0:u0], ((0, 0), (0, 16 - N_HEADS), (0, 0)))
    return {
        "norm_mix": norm_mix[:, None, :],
        "wqkv_t": w_in_t[:, :f0].astype(BF16),
        "wf_t": wf_t.astype(BF16),
        "w_u": w_in[:, :, u0:g0].astype(BF16),
        "w_g": w_in[:, :, g0:].astype(BF16),
        "b_forget": b_forget[:, :, None],
        "q_gain": jnp.tile(q_gain, (1, N_HEADS))[:, :, None],
        "k_gain": jnp.tile(k_gain, (1, N_HEADS))[:, :, None],
        "w_pool_grp": w_pool_grp.astype(BF16),
        "pool_scale": pool_scale[:, None, :],
        "w_up_attn": w_up_attn.astype(BF16),
        "w_up_pool": w_up_pool.astype(BF16),
        "w_out": w_out.astype(BF16),
        "norm_ffn": norm_ffn[:, None, :],
        "w_gate_up": w_gate_up.astype(BF16),
        "w_down": w_down.astype(BF16),
        "norm_ple": norm_ple[:, None, :],
        "w_ple_gate": w_ple_gate.astype(BF16),
        "w_ple_proj": w_ple_proj.astype(BF16),
    }


def _tile(n, preferred):
    return preferred if n % preferred == 0 else n


def kernel(x_prompt, x_sample, cache_k, cache_v, cache_logf, state_pool, page_table, p_prompt, p_sample, norm_mix, w_in, b_forget, q_gain, k_gain, w_pool_grp, pool_scale, w_up_attn, w_up_pool, w_out, norm_ffn, w_gate_up, w_down, norm_ple, w_ple_gate, w_ple_proj):
    depth = w_in.shape[0]
    b, s, d = x_prompt.shape
    n_seq = x_sample.shape[0]
    ple = p_prompt.shape[-1]
    assert x_sample.shape[1] == 1 and n_seq <= DEC_ROWS and s % LANES == 0
    w = _prepare_weights(norm_mix, w_in, b_forget, q_gain, k_gain, w_pool_grp, pool_scale, w_up_attn, w_up_pool,
                         w_out, norm_ffn, w_gate_up, w_down, norm_ple, w_ple_gate, w_ple_proj)

    cache_k_t = jnp.transpose(cache_k, (0, 1, 3, 4, 2))
    cache_v_t = jnp.transpose(cache_v, (0, 1, 3, 4, 2))
    cache_lf_t = jnp.transpose(cache_logf, (0, 1, 3, 2))
    state_t = jnp.transpose(state_pool, (0, 2, 1, 3))

    tm = _tile(s, 512)
    pad_rows = DEC_ROWS - n_seq
    xp = x_prompt
    xs = jnp.pad(x_sample.reshape(n_seq, d), ((0, pad_rows), (0, 0)))
    pp = p_prompt.reshape(depth, b * s, ple)
    ps = jnp.pad(p_sample.reshape(depth, n_seq, ple), ((0, 0), (0, pad_rows), (0, 0)))

    k_p, v_p, lf_p, pool_p, k_s, v_s, lf_s, pool_s = [], [], [], [], [], [], [], []
    for i in range(depth):
        q_t, k_t, v_t, lf_t, c_t, u, g = _inproj(xp, w, i, tm)
        attn = _prompt_attention(q_t, k_t, v_t, c_t, tm)
        xp = _mix_prompt(attn, u, g, xp, w, i, tm)
        xp = _ffn_ple(xp.reshape(b * s, d), pp, w, i, tm).reshape(b, s, d)
        k_p.append(k_t)
        v_p.append(v_t)
        lf_p.append(lf_t)
        pool_p.append(u[:, s - POOL_CTX:])

        q_t, k_t, v_t, lf_t, _, u, g = _inproj(xs[None], w, i, DEC_ROWS)
        attn_t = _decode_attention(page_table, cache_k_t, cache_v_t, cache_lf_t, q_t[0], k_t[0], v_t[0], lf_t[0], i)
        xs = _mix_decode(attn_t.T.astype(BF16), u[0], g[0], xs, state_t, w, i)
        xs = _ffn_ple(xs, ps, w, i, DEC_ROWS)
        k_s.append(k_t[0, :, :n_seq])
        v_s.append(v_t[0, :, :n_seq])
        lf_s.append(lf_t[0, :, :n_seq])
        pool_s.append(jnp.concatenate([state_t[i, 1:], u[0, None, :n_seq]], axis=0))

    def heads_last(xs_t, n_tok):
        x5 = jnp.stack(xs_t).reshape(depth, -1, N_HEADS, HEAD_DIM, n_tok)
        return jnp.transpose(x5, (0, 1, 4, 2, 3))

    k_prompt = heads_last(k_p, s)
    v_prompt = heads_last(v_p, s)
    logf_prompt = jnp.transpose(jnp.stack(lf_p), (0, 1, 3, 2))
    pool_prompt = jnp.stack(pool_p)
    k_sample = jnp.transpose(jnp.stack(k_s), (0, 2, 1)).reshape(depth, n_seq, 1, N_HEADS, HEAD_DIM)
    v_sample = jnp.transpose(jnp.stack(v_s), (0, 2, 1)).reshape(depth, n_seq, 1, N_HEADS, HEAD_DIM)
    logf_sample = jnp.transpose(jnp.stack(lf_s), (0, 2, 1))[:, :, None, :]
    pool_sample = jnp.transpose(jnp.stack(pool_s), (0, 2, 1, 3))
    y_sample = xs[:n_seq].reshape(n_seq, 1, d)
    return (xp, y_sample, k_prompt, v_prompt, logf_prompt, pool_prompt, k_sample, v_sample, logf_sample, pool_sample)
```

```python
import functools

import jax
import jax.numpy as jnp
from jax import lax
from jax.experimental import pallas as pl
from jax.experimental.pallas import tpu as pltpu

F32 = jnp.float32
BF16 = jnp.bfloat16

N_HEADS = 8
HEAD_DIM = 64
ATT_WIDTH = N_HEADS * HEAD_DIM
POOL_WINDOWS = (2, 4, 8, 16)
POOL_GROUP = 128
POOL_WIDTH = POOL_GROUP * len(POOL_WINDOWS)
POOL_CTX = max(POOL_WINDOWS) - 1
HALO = POOL_CTX + 1
RMS_EPS = 1e-6
NEG_INF = -1e30
LOG2E = 1.4426950408889634
QK_SCALE_LOG2 = HEAD_DIM ** -0.5 * LOG2E
BIAS_ROWS = 32
LANES = 128
DEC_ROWS = 128
PAGES_PER_STEP = 8
FFN_CHUNKS = 2
VMEM_LIMIT = 52 * 1024 * 1024

_NT = (((1,), (1,)), ((), ()))


def _const_spec(block, index):
    return pl.BlockSpec(block, index, pipeline_mode=pl.Buffered(1))


def _rms_rows(x, gain):
    return x * lax.rsqrt(jnp.mean(x * x, axis=-1, keepdims=True) + RMS_EPS) * gain


def _log_sigmoid(x):
    return -(jnp.maximum(-x, 0.0) + jnp.log1p(jnp.exp(-jnp.abs(x))))


def _cumsum_lanes(x):
    n = x.shape[-1]
    lane = lax.broadcasted_iota(jnp.int32, x.shape, x.ndim - 1)
    shift = 1
    while shift < n:
        x = x + jnp.where(lane >= shift, pltpu.roll(x, shift, axis=x.ndim - 1), 0.0)
        shift *= 2
    return x


def _split3(x):
    hi = x.astype(BF16).astype(F32)
    mid = (x - hi).astype(BF16).astype(F32)
    lo = (x - hi - mid).astype(BF16).astype(F32)
    return hi, mid, lo


def _inproj_kernel(x_ref, nrm_ref, wqkv_ref, wf_ref, wu_ref, wg_ref, bf_ref, qg_ref, kg_ref,
                   qT_ref, kT_ref, vT_ref, lfT_ref, u_ref, g_ref, kr_ref, kb_ref, carry_ref):
    tm = x_ref.shape[0]
    h = _rms_rows(x_ref[...], nrm_ref[...]).astype(BF16)

    def proj_t(rows):
        return lax.dot_general(rows, h, _NT, preferred_element_type=F32)

    def head_norm(y_t, gain_col):
        parts = []
        for hd in range(N_HEADS):
            y = y_t[hd * HEAD_DIM:(hd + 1) * HEAD_DIM]
            parts.append(y * lax.rsqrt(jnp.mean(y * y, axis=0, keepdims=True) + RMS_EPS))
        return jnp.concatenate(parts, axis=0) * gain_col

    q_t = head_norm(proj_t(wqkv_ref[0:ATT_WIDTH, :]), qg_ref[...])
    qT_ref[...] = (q_t * QK_SCALE_LOG2).astype(BF16)
    k_t = head_norm(proj_t(wqkv_ref[ATT_WIDTH:2 * ATT_WIDTH, :]), kg_ref[...])
    kT_ref[...] = k_t
    kr_ref[...] = k_t.T.astype(BF16)
    vT_ref[...] = proj_t(wqkv_ref[2 * ATT_WIDTH:3 * ATT_WIDTH, :])

    logf = _log_sigmoid(proj_t(wf_ref[...])[0:N_HEADS] + bf_ref[...])
    lfT_ref[...] = logf

    @pl.when(pl.program_id(1) == 0)
    def _():
        carry_ref[...] = jnp.zeros_like(carry_ref)

    c = _cumsum_lanes(logf) + carry_ref[:, 0:1]
    carry_ref[...] = jnp.broadcast_to(c[:, tm - 1:tm], carry_ref.shape)

    terms = _split3(c * (-LOG2E))
    row = lax.broadcasted_iota(jnp.int32, (BIAS_ROWS, tm), 0)
    bias_t = jnp.zeros((BIAS_ROWS, tm), F32)
    for hd in range(N_HEADS):
        for term in range(3):
            bias_t = jnp.where(row == 3 * hd + term, terms[term][hd:hd + 1], bias_t)
    bias_t = jnp.concatenate([bias_t, jnp.zeros((LANES - BIAS_ROWS, tm), F32)], axis=0)
    kb_ref[...] = bias_t.T.astype(BF16)

    u_ref[...] = jnp.dot(h, wu_ref[...], preferred_element_type=F32)
    g_ref[...] = jnp.dot(h, wg_ref[...], preferred_element_type=F32)


def _inproj(x, w, layer, tm):
    b, s, d = x.shape
    lay = lambda *_: (layer, 0, 0)
    tok_t = lambda rows: pl.BlockSpec((None, rows, tm), lambda i, j: (i, 0, j))
    tok = lambda cols: pl.BlockSpec((None, tm, cols), lambda i, j: (i, j, 0))
    return pl.pallas_call(
        _inproj_kernel,
        grid=(b, s // tm),
        in_specs=[
            tok(d),
            _const_spec((None, 1, d), lay),
            _const_spec((None, 3 * ATT_WIDTH, d), lay),
            _const_spec((None, 16, d), lay),
            _const_spec((None, d, POOL_WIDTH), lay),
            _const_spec((None, d, 2 * d), lay),
            _const_spec((None, N_HEADS, 1), lay),
            _const_spec((None, ATT_WIDTH, 1), lay),
            _const_spec((None, ATT_WIDTH, 1), lay),
        ],
        out_specs=[tok_t(ATT_WIDTH), tok_t(ATT_WIDTH), tok_t(ATT_WIDTH), tok_t(N_HEADS),
                   tok(POOL_WIDTH), tok(2 * d), tok(ATT_WIDTH), tok(LANES)],
        out_shape=[
            jax.ShapeDtypeStruct((b, ATT_WIDTH, s), BF16),
            jax.ShapeDtypeStruct((b, ATT_WIDTH, s), F32),
            jax.ShapeDtypeStruct((b, ATT_WIDTH, s), F32),
            jax.ShapeDtypeStruct((b, N_HEADS, s), F32),
            jax.ShapeDtypeStruct((b, s, POOL_WIDTH), F32),
            jax.ShapeDtypeStruct((b, s, 2 * d), F32),
            jax.ShapeDtypeStruct((b, s, ATT_WIDTH), BF16),
            jax.ShapeDtypeStruct((b, s, LANES), BF16),
        ],
        scratch_shapes=[pltpu.VMEM((N_HEADS, LANES), F32)],
        compiler_params=pltpu.CompilerParams(dimension_semantics=("parallel", "arbitrary"),
                                             vmem_limit_bytes=VMEM_LIMIT),
        name="inproj",
    )(x, w["norm_mix"], w["wqkv_t"], w["wf_t"], w["w_u"], w["w_g"], w["b_forget"], w["q_gain"], w["k_gain"])


def _attn_kernel(qi_tab, kj_tab, qT_ref, kr_ref, kb_ref, vT_ref, o_ref, qbd_ref, va_ref, m_ref, acc_ref):
    tq = o_ref.shape[0]
    tk = kr_ref.shape[0]
    pair = pl.program_id(1)
    t = pl.program_id(2)
    qi = qi_tab[t]
    kj = kj_tab[t]

    @pl.when(kj == 0)
    def _():
        q = qT_ref[...]
        top = lax.broadcasted_iota(jnp.int32, q.shape, 0) < HEAD_DIM
        zero = jnp.zeros_like(q)
        qbd_ref[0:LANES] = jnp.concatenate([jnp.where(top, q, zero), jnp.where(top, zero, q)], axis=1)
        row = lax.broadcasted_iota(jnp.int32, (LANES, 2 * tq), 0) - 6 * pair
        second = lax.broadcasted_iota(jnp.int32, (LANES, 2 * tq), 1) >= tq
        lo = jnp.where(second, 3, 0)
        qbd_ref[LANES:2 * LANES] = jnp.where(jnp.logical_and(row >= lo, row < lo + 3), 1.0, 0.0).astype(BF16)
        ones_row = lax.broadcasted_iota(jnp.int32, (16, tk), 0) == 0
        va_ref[LANES:LANES + 16] = jnp.where(ones_row, 1.0, 0.0).astype(BF16)
        m_ref[...] = jnp.full_like(m_ref, NEG_INF)
        acc_ref[...] = jnp.zeros_like(acc_ref)

    def step(on_diagonal):
        keys = jnp.concatenate([kr_ref[...], kb_ref[...]], axis=1)
        s = jnp.dot(keys, qbd_ref[...], preferred_element_type=F32)
        if on_diagonal:
            key_pos = lax.broadcasted_iota(jnp.int32, (tk, tq), 0)
            qry_pos = lax.broadcasted_iota(jnp.int32, (tk, tq), 1)
            allowed = key_pos <= qry_pos
            s = jnp.where(jnp.concatenate([allowed, allowed], axis=1), s, NEG_INF)
        m_prev = m_ref[0:1]
        m_new = jnp.maximum(m_prev, jnp.max(s, axis=0, keepdims=True))
        alpha = jnp.exp2(m_prev - m_new)
        p = jnp.exp2(s - m_new).astype(BF16)
        va_ref[0:LANES] = vT_ref[...].astype(BF16)
        acc_ref[...] = alpha * acc_ref[...] + jnp.dot(va_ref[...], p, preferred_element_type=F32)
        m_ref[...] = jnp.broadcast_to(m_new, m_ref.shape)

    @pl.when(kj < qi)
    def _():
        step(False)

    @pl.when(kj == qi)
    def _():
        step(True)
        first = acc_ref[0:HEAD_DIM, 0:tq] / acc_ref[LANES:LANES + 1, 0:tq]
        second = acc_ref[HEAD_DIM:LANES, tq:2 * tq] / acc_ref[LANES:LANES + 1, tq:2 * tq]
        o_ref[...] = jnp.concatenate([first, second], axis=0).T.astype(o_ref.dtype)


def _prompt_attention(q_t, k_rows, k_bias, v_t, tile):
    b, _, s = q_t.shape
    n = s // tile
    qi_tab = jnp.asarray([i for i in range(n) for _ in range(i + 1)], jnp.int32)
    kj_tab = jnp.asarray([j for i in range(n) for j in range(i + 1)], jnp.int32)
    pairs = N_HEADS // 2
    grid_spec = pltpu.PrefetchScalarGridSpec(
        num_scalar_prefetch=2,
        grid=(b, pairs, n * (n + 1) // 2),
        in_specs=[
            pl.BlockSpec((None, LANES, tile), lambda i, p, t, qi, kj: (i, p, qi[t])),
            pl.BlockSpec((None, tile, LANES), lambda i, p, t, qi, kj: (i, kj[t], p)),
            pl.BlockSpec((None, tile, LANES), lambda i, p, t, qi, kj: (i, kj[t], 0)),
            pl.BlockSpec((None, LANES, tile), lambda i, p, t, qi, kj: (i, p, kj[t])),
        ],
        out_specs=pl.BlockSpec((None, tile, LANES), lambda i, p, t, qi, kj: (i, qi[t], p)),
        scratch_shapes=[
            pltpu.VMEM((2 * LANES, 2 * tile), BF16),
            pltpu.VMEM((LANES + 16, tile), BF16),
            pltpu.VMEM((8, 2 * tile), F32),
            pltpu.VMEM((LANES + 16, 2 * tile), F32),
        ],
    )
    return pl.pallas_call(
        _attn_kernel,
        grid_spec=grid_spec,
        out_shape=jax.ShapeDtypeStruct((b, s, ATT_WIDTH), BF16),
        compiler_params=pltpu.CompilerParams(dimension_semantics=("parallel", "parallel", "arbitrary"),
                                             vmem_limit_bytes=VMEM_LIMIT),
        name="prompt_attention",
    )(qi_tab, kj_tab, q_t, k_rows, k_bias, v_t)


def _sum_heads(parts):
    sub = lax.broadcasted_iota(jnp.int32, parts[0].shape, 0)
    x = [parts[h] for h in (0, 4, 2, 6, 1, 5, 3, 7)]
    x = [v + pltpu.roll(v, 4, axis=0) for v in x]
    x = [jnp.where(sub < 4, x[2 * i], x[2 * i + 1]) for i in range(4)]
    x = [v + pltpu.roll(v, 6, axis=0) for v in x]
    x = [jnp.where((sub & 2) == 0, x[2 * i], pltpu.roll(x[2 * i + 1], 2, axis=0)) for i in range(2)]
    x = [v + pltpu.roll(v, 7, axis=0) for v in x]
    return jnp.where((sub & 1) == 0, x[0], pltpu.roll(x[1], 1, axis=0))


def _decode_attn_kernel(pt_ref, *refs):
    g_pages = PAGES_PER_STEP
    k_refs = refs[0:g_pages]
    v_refs = refs[g_pages:2 * g_pages]
    lf_refs = refs[2 * g_pages:3 * g_pages]
    (qT_ref, knT_ref, vnT_ref, lfnT_ref, scan_ref, o_ref,
     q_ref, kn_ref, vn_ref, lfn_ref, m_ref, l_ref, carry_ref, acc_ref) = refs[3 * g_pages:]
    seq = pl.program_id(0)
    grp = pl.program_id(1)
    last = pl.num_programs(1) - 1

    def column(ref):
        x = ref[...].astype(F32)
        lane = lax.broadcasted_iota(jnp.int32, x.shape, 1)
        return jnp.sum(jnp.where(lane == seq, x, 0.0), axis=1, keepdims=True)

    @pl.when(jnp.logical_and(seq == 0, grp == 0))
    def _():
        o_ref[...] = jnp.zeros_like(o_ref)

    @pl.when(grp == 0)
    def _():
        q_ref[...] = jnp.broadcast_to(column(qT_ref), q_ref.shape)
        kn_ref[...] = column(knT_ref)
        vn_ref[...] = column(vnT_ref)
        lfn_ref[...] = column(lfnT_ref)
        m_ref[...] = jnp.full_like(m_ref, NEG_INF)
        l_ref[...] = jnp.zeros_like(l_ref)
        carry_ref[...] = jnp.zeros_like(carry_ref)
        acc_ref[...] = jnp.zeros_like(acc_ref)

    n_rows = N_HEADS * g_pages
    lf = jnp.concatenate([r[...] for r in lf_refs], axis=0)
    terms = jnp.concatenate([x.astype(BF16) for x in _split3(lf)], axis=0)
    sums = jnp.dot(terms, scan_ref[...], preferred_element_type=F32)
    sums = sums[0:n_rows] + sums[n_rows:2 * n_rows] + sums[2 * n_rows:3 * n_rows]

    def head_scores(prod):
        parts = []
        for hd in range(N_HEADS):
            part = prod(hd, 0)
            for r in range(1, HEAD_DIM // 8):
                part = part + prod(hd, r)
            parts.append(part)
        return _sum_heads(parts)

    carry = carry_ref[...]
    s_pages = []
    for g in range(g_pages):
        c = sums[8 * g:8 * g + 8, 0:LANES] + carry
        carry = carry + sums[8 * g:8 * g + 8, LANES:2 * LANES]
        s = head_scores(lambda hd, r, g=g: k_refs[g][hd, 8 * r:8 * r + 8] * q_ref[hd * HEAD_DIM + 8 * r:hd * HEAD_DIM + 8 * r + 8])
        s_pages.append(s - c * LOG2E)
    carry_ref[...] = carry

    m_prev = m_ref[...]
    m_new = m_prev
    for s in s_pages:
        m_new = jnp.maximum(m_new, s)
    alpha = jnp.exp2(m_prev - m_new)
    p_pages = [jnp.exp2(s - m_new) for s in s_pages]
    l_new = alpha * l_ref[...]
    for p in p_pages:
        l_new = l_new + p
    l_ref[...] = l_new
    m_ref[...] = m_new
    for hd in range(N_HEADS):
        acc = alpha[hd:hd + 1] * acc_ref[hd]
        for g in range(g_pages):
            acc = acc + p_pages[g][hd:hd + 1] * v_refs[g][hd]
        acc_ref[hd] = acc

    @pl.when(grp == last)
    def _():
        kn = kn_ref[...]
        s_new = head_scores(lambda hd, r: q_ref[hd * HEAD_DIM + 8 * r:hd * HEAD_DIM + 8 * r + 8]
                            * kn[hd * HEAD_DIM + 8 * r:hd * HEAD_DIM + 8 * r + 8])
        s_new = s_new - (carry_ref[...] + lfn_ref[...]) * LOG2E
        m_lane = m_ref[...]
        m_fin = jnp.maximum(jnp.max(m_lane, axis=1, keepdims=True), s_new)
        w = jnp.exp2(m_lane - m_fin)
        p_new = jnp.exp2(s_new - m_fin)
        l_fin = jnp.sum(l_ref[...] * w, axis=1, keepdims=True) + p_new
        outs = []
        for hd in range(N_HEADS):
            rows = slice(hd * HEAD_DIM, (hd + 1) * HEAD_DIM)
            past = jnp.sum(acc_ref[hd] * w[hd:hd + 1], axis=1, keepdims=True)
            outs.append((past + p_new[hd:hd + 1] * vn_ref[rows]) / l_fin[hd:hd + 1])
        out = jnp.concatenate(outs, axis=0)
        lane = lax.broadcasted_iota(jnp.int32, o_ref.shape, 1)
        o_ref[...] = jnp.where(lane == seq, out, o_ref[...])


def _decode_attention(page_table, cache_k_t, cache_v_t, cache_lf_t, q_t, kn_t, vn_t, lfn_t, layer):
    n_seq, n_pages = page_table.shape
    g_pages = PAGES_PER_STEP
    pt = page_table.reshape(-1)

    def page(g):
        return lambda b, j, pt_ref: (layer, pt_ref[b * n_pages + j * g_pages + g], 0, 0, 0)

    def page_lf(g):
        return lambda b, j, pt_ref: (layer, pt_ref[b * n_pages + j * g_pages + g], 0, 0)

    kv_block = (None, None, N_HEADS, HEAD_DIM, LANES)
    whole = lambda rows, cols=LANES: pl.BlockSpec((rows, cols), lambda b, j, pt_ref: (0, 0))
    lane_ge_row = lax.broadcasted_iota(jnp.int32, (LANES, LANES), 1) >= lax.broadcasted_iota(jnp.int32, (LANES, LANES), 0)
    scan = jnp.concatenate([lane_ge_row.astype(BF16), jnp.ones((LANES, LANES), BF16)], axis=1)
    grid_spec = pltpu.PrefetchScalarGridSpec(
        num_scalar_prefetch=1,
        grid=(n_seq, n_pages // g_pages),
        in_specs=([pl.BlockSpec(kv_block, page(g)) for g in range(g_pages)]
                  + [pl.BlockSpec(kv_block, page(g)) for g in range(g_pages)]
                  + [pl.BlockSpec((None, None, N_HEADS, LANES), page_lf(g)) for g in range(g_pages)]
                  + [whole(ATT_WIDTH), whole(ATT_WIDTH), whole(ATT_WIDTH), whole(N_HEADS), whole(LANES, 2 * LANES)]),
        out_specs=whole(ATT_WIDTH),
        scratch_shapes=[
            pltpu.VMEM((ATT_WIDTH, LANES), F32),
            pltpu.VMEM((ATT_WIDTH, 1), F32),
            pltpu.VMEM((ATT_WIDTH, 1), F32),
            pltpu.VMEM((N_HEADS, 1), F32),
            pltpu.VMEM((N_HEADS, LANES), F32),
            pltpu.VMEM((N_HEADS, LANES), F32),
            pltpu.VMEM((N_HEADS, LANES), F32),
            pltpu.VMEM((N_HEADS, HEAD_DIM, LANES), F32),
        ],
    )
    return pl.pallas_call(
        _decode_attn_kernel,
        grid_spec=grid_spec,
        out_shape=jax.ShapeDtypeStruct((ATT_WIDTH, LANES), F32),
        compiler_params=pltpu.CompilerParams(dimension_semantics=("arbitrary", "arbitrary"),
                                             vmem_limit_bytes=VMEM_LIMIT),
        name="decode_attention",
    )(pt, *([cache_k_t] * g_pages), *([cache_v_t] * g_pages), *([cache_lf_t] * g_pages), q_t, kn_t, vn_t, lfn_t, scan)


def _mix_tail(d_groups, attn, gates, x, wp_ref, ps_ref, wua_ref, wup_ref, wo_ref):
    pooled = []
    for g in range(len(POOL_WINDOWS)):
        y = jnp.dot(d_groups[g].astype(BF16), wp_ref[g], preferred_element_type=F32)
        pooled.append((y * ps_ref[:, g * POOL_GROUP:(g + 1) * POOL_GROUP]).astype(BF16))
    pool = jnp.concatenate(pooled, axis=1)
    d_model = x.shape[1]
    a = jnp.dot(attn, wua_ref[...], preferred_element_type=F32)
    b = jnp.dot(pool, wup_ref[...], preferred_element_type=F32)
    mixed = jax.nn.sigmoid(gates[:, 0:d_model]) * a + jax.nn.sigmoid(gates[:, d_model:2 * d_model]) * b
    return x + jnp.dot(mixed.astype(BF16), wo_ref[...], preferred_element_type=F32)


def _mix_prompt_kernel(attn_ref, u_ref, g_ref, x_ref, wp_ref, ps_ref, wua_ref, wup_ref, wo_ref, o_ref, ext_ref):
    tm = u_ref.shape[0]
    j = pl.program_id(1)

    @pl.when(j == 0)
    def _():
        ext_ref[0:HALO] = jnp.zeros((HALO, POOL_WIDTH), F32)

    @pl.when(j > 0)
    def _():
        ext_ref[0:HALO] = ext_ref[tm:tm + HALO]

    ext_ref[HALO:HALO + tm] = u_ref[...]
    pos = j * tm + lax.broadcasted_iota(jnp.int32, (tm, 1), 0)
    d_groups = []
    for g, w in enumerate(POOL_WINDOWS):
        lanes = slice(g * POOL_GROUP, (g + 1) * POOL_GROUP)
        win = ext_ref[HALO:HALO + tm, lanes]
        for back in range(1, w):
            win = win + ext_ref[HALO - back:HALO - back + tm, lanes]
        cnt = jnp.minimum(pos + 1, w).astype(F32)
        d_groups.append(win / cnt - ext_ref[HALO:HALO + tm, lanes])
    o_ref[...] = _mix_tail(d_groups, attn_ref[...], g_ref[...], x_ref[...], wp_ref, ps_ref, wua_ref, wup_ref, wo_ref)


def _mix_decode_kernel(attn_ref, u_ref, g_ref, x_ref, st_ref, wp_ref, ps_ref, wua_ref, wup_ref, wo_ref, o_ref):
    tm = u_ref.shape[0]
    n_seq = st_ref.shape[1]
    d_groups = []
    for g, w in enumerate(POOL_WINDOWS):
        lanes = slice(g * POOL_GROUP, (g + 1) * POOL_GROUP)
        u_new = u_ref[0:n_seq, lanes]
        win = u_new
        for back in range(1, w):
            win = win + st_ref[POOL_CTX - back, :, lanes]
        d = win / float(w) - u_new
        d_groups.append(jnp.concatenate([d, jnp.zeros((tm - n_seq, POOL_GROUP), F32)], axis=0))
    o_ref[...] = _mix_tail(d_groups, attn_ref[...], g_ref[...], x_ref[...], wp_ref, ps_ref, wua_ref, wup_ref, wo_ref)


def _mix_weight_specs(w, layer, d):
    lay3 = lambda *_: (layer, 0, 0)
    lay4 = lambda *_: (layer, 0, 0, 0)
    specs = [
        _const_spec((None, len(POOL_WINDOWS), POOL_GROUP, POOL_GROUP), lay4),
        _const_spec((None, 1, POOL_WIDTH), lay3),
        _const_spec((None, ATT_WIDTH, d), lay3),
        _const_spec((None, POOL_WIDTH, d), lay3),
        _const_spec((None, d, d), lay3),
    ]
    return specs, (w["w_pool_grp"], w["pool_scale"], w["w_up_attn"], w["w_up_pool"], w["w_out"])


def _mix_prompt(attn, u, g, x, w, layer, tm):
    b, s, d = x.shape
    tok = lambda cols: pl.BlockSpec((None, tm, cols), lambda i, j: (i, j, 0))
    w_specs, w_args = _mix_weight_specs(w, layer, d)
    return pl.pallas_call(
        _mix_prompt_kernel,
        grid=(b, s // tm),
        in_specs=[tok(ATT_WIDTH), tok(POOL_WIDTH), tok(2 * d), tok(d)] + w_specs,
        out_specs=tok(d),
        out_shape=jax.ShapeDtypeStruct((b, s, d), F32),
        scratch_shapes=[pltpu.VMEM((HALO + tm, POOL_WIDTH), F32)],
        compiler_params=pltpu.CompilerParams(dimension_semantics=("parallel", "arbitrary"),
                                             vmem_limit_bytes=VMEM_LIMIT),
        name="mix_prompt",
    )(attn, u, g, x, *w_args)


def _mix_decode(attn, u, g, x, state_t, w, layer):
    rows, d = x.shape
    n_seq = state_t.shape[2]
    full = lambda cols: pl.BlockSpec((rows, cols), lambda i: (0, 0))
    w_specs, w_args = _mix_weight_specs(w, layer, d)
    return pl.pallas_call(
        _mix_decode_kernel,
        grid=(1,),
        in_specs=[full(ATT_WIDTH), full(POOL_WIDTH), full(2 * d), full(d),
                  pl.BlockSpec((None, POOL_CTX, n_seq, POOL_WIDTH), lambda i: (layer, 0, 0, 0))] + w_specs,
        out_specs=full(d),
        out_shape=jax.ShapeDtypeStruct((rows, d), F32),
        compiler_params=pltpu.CompilerParams(dimension_semantics=("arbitrary",), vmem_limit_bytes=VMEM_LIMIT),
        name="mix_decode",
    )(attn, u, g, x, state_t, *w_args)


def _ffn_ple_kernel(x_ref, p_ref, nf_ref, wgate_ref, wup_ref, wdown_ref, np_ref, wpg_ref, wpp_ref, o_ref):
    x = x_ref[...]
    h = _rms_rows(x, nf_ref[...]).astype(BF16)
    hidden = wgate_ref.shape[1]
    chunk = hidden // FFN_CHUNKS
    y = x
    for c in range(FFN_CHUNKS):
        cols = slice(c * chunk, (c + 1) * chunk)
        gate = jnp.dot(h, wgate_ref[:, cols], preferred_element_type=F32)
        up = jnp.dot(h, wup_ref[:, cols], preferred_element_type=F32)
        act = (jax.nn.silu(gate) * up).astype(BF16)
        y = y + jnp.dot(act, wdown_ref[cols, :], preferred_element_type=F32)
    h2 = _rms_rows(y, np_ref[...]).astype(BF16)
    gate2 = jax.nn.sigmoid(jnp.dot(h2, wpg_ref[...], preferred_element_type=F32))
    emb = jnp.dot(p_ref[...].astype(BF16), wpp_ref[...], preferred_element_type=F32)
    o_ref[...] = y + gate2 * emb


def _ffn_ple(x, p_all, w, layer, tm):
    m, d = x.shape
    ple = p_all.shape[2]
    hidden = w["w_down"].shape[1]
    lay = lambda *_: (layer, 0, 0)
    return pl.pallas_call(
        _ffn_ple_kernel,
        grid=(m // tm,),
        in_specs=[
            pl.BlockSpec((tm, d), lambda i: (i, 0)),
            pl.BlockSpec((None, tm, ple), lambda i: (layer, i, 0)),
            _const_spec((None, 1, d), lay),
            _const_spec((None, d, hidden), lay),
            _const_spec((None, d, hidden), lambda *_: (layer, 0, 1)),
            _const_spec((None, hidden, d), lay),
            _const_spec((None, 1, d), lay),
            _const_spec((None, d, d), lay),
            _const_spec((None, ple, d), lay),
        ],
        out_specs=pl.BlockSpec((tm, d), lambda i: (i, 0)),
        out_shape=jax.ShapeDtypeStruct((m, d), F32),
        compiler_params=pltpu.CompilerParams(dimension_semantics=("parallel",), vmem_limit_bytes=VMEM_LIMIT),
        name="ffn_ple",
    )(x, p_all, w["norm_ffn"], w["w_gate_up"], w["w_gate_up"], w["w_down"], w["norm_ple"], w["w_ple_gate"], w["w_ple_proj"])


def _prepare_weights(norm_mix, w_in, b_forget, q_gain, k_gain, w_pool_grp, pool_scale, w_up_attn, w_up_pool,
                     w_out, norm_ffn, w_gate_up, w_down, norm_ple, w_ple_gate, w_ple_proj):
    a = ATT_WIDTH
    f0, u0, g0 = 3 * a, 3 * a + N_HEADS, 3 * a + N_HEADS + POOL_WIDTH
    w_in_t = jnp.swapaxes(w_in, 1, 2)
    wf_t = jnp.pad(w_in_t[:, f0:u0], ((0, 0), (0, 16 - N_HEADS), (0, 0)))
    return {
        "norm_mix": norm_mix[:, None, :],
        "wqkv_t": w_in_t[:, :f0].astype(BF16),
        "wf_t": wf_t.astype(BF16),
        "w_u": w_in[:, :, u0:g0].astype(BF16),
        "w_g": w_in[:, :, g0:].astype(BF16),
        "b_forget": b_forget[:, :, None],
        "q_gain": jnp.tile(q_gain, (1, N_HEADS))[:, :, None],
        "k_gain": jnp.tile(k_gain, (1, N_HEADS))[:, :, None],
        "w_pool_grp": w_pool_grp.astype(BF16),
        "pool_scale": pool_scale[:, None, :],
        "w_up_attn": w_up_attn.astype(BF16),
        "w_up_pool": w_up_pool.astype(BF16),
        "w_out": w_out.astype(BF16),
        "norm_ffn": norm_ffn[:, None, :],
        "w_gate_up": w_gate_up.astype(BF16),
        "w_down": w_down.astype(BF16),
        "norm_ple": norm_ple[:, None, :],
        "w_ple_gate": w_ple_gate.astype(BF16),
        "w_ple_proj": w_ple_proj.astype(BF16),
    }


def _tile(n, preferred):
    return preferred if n % preferred == 0 else n


def kernel(x_prompt, x_sample, cache_k, cache_v, cache_logf, state_pool, page_table, p_prompt, p_sample, norm_mix, w_in, b_forget, q_gain, k_gain, w_pool_grp, pool_scale, w_up_attn, w_up_pool, w_out, norm_ffn, w_gate_up, w_down, norm_ple, w_ple_gate, w_ple_proj):
    depth = w_in.shape[0]
    b, s, d = x_prompt.shape
    n_seq = x_sample.shape[0]
    ple = p_prompt.shape[-1]
    assert x_sample.shape[1] == 1 and n_seq <= DEC_ROWS and s % LANES == 0
    w = _prepare_weights(norm_mix, w_in, b_forget, q_gain, k_gain, w_pool_grp, pool_scale, w_up_attn, w_up_pool,
                         w_out, norm_ffn, w_gate_up, w_down, norm_ple, w_ple_gate, w_ple_proj)

    cache_k_t = jnp.transpose(cache_k, (0, 1, 3, 4, 2))
    cache_v_t = jnp.transpose(cache_v, (0, 1, 3, 4, 2))
    cache_lf_t = jnp.transpose(cache_logf, (0, 1, 3, 2))
    state_t = jnp.transpose(state_pool, (0, 2, 1, 3))

    tm = _tile(s, 512)
    pad_rows = DEC_ROWS - n_seq
    xp = x_prompt
    xs = jnp.pad(x_sample.reshape(n_seq, d), ((0, pad_rows), (0, 0)))
    pp = p_prompt.reshape(depth, b * s, ple)
    ps = jnp.pad(p_sample.reshape(depth, n_seq, ple), ((0, 0), (0, pad_rows), (0, 0)))

    k_p, v_p, lf_p, pool_p, k_s, v_s, lf_s, pool_s = [], [], [], [], [], [], [], []
    for i in range(depth):
        q_t, k_t, v_t, lf_t, u, g, k_rows, k_bias = _inproj(xp, w, i, tm)
        attn = _prompt_attention(q_t, k_rows, k_bias, v_t, tm)
        xp = _mix_prompt(attn, u, g, xp, w, i, tm)
        xp = _ffn_ple(xp.reshape(b * s, d), pp, w, i, tm).reshape(b, s, d)
        k_p.append(k_t)
        v_p.append(v_t)
        lf_p.append(lf_t)
        pool_p.append(u[:, s - POOL_CTX:])

        q_t, k_t, v_t, lf_t, u, g, _, _ = _inproj(xs[None], w, i, DEC_ROWS)
        attn_t = _decode_attention(page_table, cache_k_t, cache_v_t, cache_lf_t, q_t[0], k_t[0], v_t[0], lf_t[0], i)
        xs = _mix_decode(attn_t.T.astype(BF16), u[0], g[0], xs, state_t, w, i)
        xs = _ffn_ple(xs, ps, w, i, DEC_ROWS)
        k_s.append(k_t[0, :, :n_seq])
        v_s.append(v_t[0, :, :n_seq])
        lf_s.append(lf_t[0, :, :n_seq])
        pool_s.append(jnp.concatenate([state_t[i, 1:], u[0, None, :n_seq]], axis=0))

    def heads_last(xs_t, n_tok):
        x5 = jnp.stack(xs_t).reshape(depth, -1, N_HEADS, HEAD_DIM, n_tok)
        return jnp.transpose(x5, (0, 1, 4, 2, 3))

    k_prompt = heads_last(k_p, s)
    v_prompt = heads_last(v_p, s)
    logf_prompt = jnp.transpose(jnp.stack(lf_p), (0, 1, 3, 2))
    pool_prompt = jnp.stack(pool_p)
    k_sample = jnp.transpose(jnp.stack(k_s), (0, 2, 1)).reshape(depth, n_seq, 1, N_HEADS, HEAD_DIM)
    v_sample = jnp.transpose(jnp.stack(v_s), (0, 2, 1)).reshape(depth, n_seq, 1, N_HEADS, HEAD_DIM)
    logf_sample = jnp.transpose(jnp.stack(lf_s), (0, 2, 1))[:, :, None, :]
    pool_sample = jnp.transpose(jnp.stack(pool_s), (0, 2, 1, 3))
    y_sample = xs[:n_seq].reshape(n_seq, 1, d)
    return (xp, y_sample, k_prompt, v_prompt, logf_prompt, pool_prompt, k_sample, v_sample, logf_sample, pool_sample)
```

```python
import functools
from typing import Any, NamedTuple

import jax
import jax.numpy as jnp
from jax import lax
from jax.experimental import pallas as pl
from jax.experimental.pallas import tpu as pltpu

F32 = jnp.float32
BF16 = jnp.bfloat16

N_HEADS = 8
HEAD_DIM = 64
ATT_WIDTH = N_HEADS * HEAD_DIM
POOL_WINDOWS = (2, 4, 8, 16)
POOL_GROUP = 128
POOL_WIDTH = POOL_GROUP * len(POOL_WINDOWS)
POOL_CTX = max(POOL_WINDOWS) - 1
HALO = POOL_CTX + 1
RMS_EPS = 1e-6
NEG_INF = -1e30
LOG2E = 1.4426950408889634
QK_SCALE_LOG2 = HEAD_DIM ** -0.5 * LOG2E
BIAS_ROWS = 32
LANES = 128
DEC_ROWS = 128
FFN_CHUNKS = 2
VMEM_LIMIT = 52 * 1024 * 1024

_NT = (((1,), (1,)), ((), ()))


def _const_spec(block, index):
    return pl.BlockSpec(block, index, pipeline_mode=pl.Buffered(1))


def _rms_rows(x, gain):
    return x * lax.rsqrt(jnp.mean(x * x, axis=-1, keepdims=True) + RMS_EPS) * gain


def _log_sigmoid(x):
    return -(jnp.maximum(-x, 0.0) + jnp.log1p(jnp.exp(-jnp.abs(x))))


def _cumsum_lanes(x):
    n = x.shape[-1]
    lane = lax.broadcasted_iota(jnp.int32, x.shape, x.ndim - 1)
    shift = 1
    while shift < n:
        x = x + jnp.where(lane >= shift, pltpu.roll(x, shift, axis=x.ndim - 1), 0.0)
        shift *= 2
    return x


def _split3(x):
    hi = x.astype(BF16).astype(F32)
    mid = (x - hi).astype(BF16).astype(F32)
    lo = (x - hi - mid).astype(BF16).astype(F32)
    return hi, mid, lo


def _inproj_kernel(x_ref, nrm_ref, wqkv_ref, wf_ref, wu_ref, wg_ref, bf_ref, qg_ref, kg_ref,
                   qT_ref, kT_ref, vT_ref, lfT_ref, u_ref, g_ref, kr_ref, kb_ref, carry_ref):
    tm = x_ref.shape[0]
    h = _rms_rows(x_ref[...], nrm_ref[...]).astype(BF16)

    def proj_t(rows):
        return lax.dot_general(rows, h, _NT, preferred_element_type=F32)

    def head_norm(y_t, gain_col):
        parts = []
        for hd in range(N_HEADS):
            y = y_t[hd * HEAD_DIM:(hd + 1) * HEAD_DIM]
            parts.append(y * lax.rsqrt(jnp.mean(y * y, axis=0, keepdims=True) + RMS_EPS))
        return jnp.concatenate(parts, axis=0) * gain_col

    q_t = head_norm(proj_t(wqkv_ref[0:ATT_WIDTH, :]), qg_ref[...])
    qT_ref[...] = (q_t * QK_SCALE_LOG2).astype(BF16)
    k_t = head_norm(proj_t(wqkv_ref[ATT_WIDTH:2 * ATT_WIDTH, :]), kg_ref[...])
    kT_ref[...] = k_t
    kr_ref[...] = k_t.T.astype(BF16)
    vT_ref[...] = proj_t(wqkv_ref[2 * ATT_WIDTH:3 * ATT_WIDTH, :])

    logf = _log_sigmoid(proj_t(wf_ref[...])[0:N_HEADS] + bf_ref[...])
    lfT_ref[...] = logf

    @pl.when(pl.program_id(1) == 0)
    def _():
        carry_ref[...] = jnp.zeros_like(carry_ref)

    c = _cumsum_lanes(logf) + carry_ref[:, 0:1]
    carry_ref[...] = jnp.broadcast_to(c[:, tm - 1:tm], carry_ref.shape)

    terms = _split3(c * (-LOG2E))
    row = lax.broadcasted_iota(jnp.int32, (BIAS_ROWS, tm), 0)
    bias_t = jnp.zeros((BIAS_ROWS, tm), F32)
    for hd in range(N_HEADS):
        for term in range(3):
            bias_t = jnp.where(row == 3 * hd + term, terms[term][hd:hd + 1], bias_t)
    bias_t = jnp.concatenate([bias_t, jnp.zeros((LANES - BIAS_ROWS, tm), F32)], axis=0)
    kb_ref[...] = bias_t.T.astype(BF16)

    u_ref[...] = jnp.dot(h, wu_ref[...], preferred_element_type=F32)
    g_ref[...] = jnp.dot(h, wg_ref[...], preferred_element_type=F32).astype(g_ref.dtype)


def _inproj(x, w, layer, tm):
    b, s, d = x.shape
    lay = lambda *_: (layer, 0, 0)
    tok_t = lambda rows: pl.BlockSpec((None, rows, tm), lambda i, j: (i, 0, j))
    tok = lambda cols: pl.BlockSpec((None, tm, cols), lambda i, j: (i, j, 0))
    return pl.pallas_call(
        _inproj_kernel,
        grid=(b, s // tm),
        in_specs=[
            tok(d),
            _const_spec((None, 1, d), lay),
            _const_spec((None, 3 * ATT_WIDTH, d), lay),
            _const_spec((None, 16, d), lay),
            _const_spec((None, d, POOL_WIDTH), lay),
            _const_spec((None, d, 2 * d), lay),
            _const_spec((None, N_HEADS, 1), lay),
            _const_spec((None, ATT_WIDTH, 1), lay),
            _const_spec((None, ATT_WIDTH, 1), lay),
        ],
        out_specs=[tok_t(ATT_WIDTH), tok_t(ATT_WIDTH), tok_t(ATT_WIDTH), tok_t(N_HEADS),
                   tok(POOL_WIDTH), tok(2 * d), tok(ATT_WIDTH), tok(LANES)],
        out_shape=[
            jax.ShapeDtypeStruct((b, ATT_WIDTH, s), BF16),
            jax.ShapeDtypeStruct((b, ATT_WIDTH, s), F32),
            jax.ShapeDtypeStruct((b, ATT_WIDTH, s), F32),
            jax.ShapeDtypeStruct((b, N_HEADS, s), F32),
            jax.ShapeDtypeStruct((b, s, POOL_WIDTH), F32),
            jax.ShapeDtypeStruct((b, s, 2 * d), BF16),
            jax.ShapeDtypeStruct((b, s, ATT_WIDTH), BF16),
            jax.ShapeDtypeStruct((b, s, LANES), BF16),
        ],
        scratch_shapes=[pltpu.VMEM((N_HEADS, LANES), F32)],
        compiler_params=pltpu.CompilerParams(dimension_semantics=("parallel", "arbitrary"),
                                             vmem_limit_bytes=VMEM_LIMIT),
        name="inproj",
    )(x, w["norm_mix"], w["wqkv_t"], w["wf_t"], w["w_u"], w["w_g"], w["b_forget"], w["q_gain"], w["k_gain"])


class _Decode(NamedTuple):
    k: Any
    v: Any
    lf: Any
    q_t: Any
    kn_t: Any
    vn_t: Any
    lfn_t: Any
    scan: Any
    out: Any
    q: Any
    kn: Any
    vn: Any
    lfn: Any
    m: Any
    l: Any
    carry: Any
    acc: Any


def _sum_heads(parts):
    sub = lax.broadcasted_iota(jnp.int32, parts[0].shape, 0)
    x = [parts[h] for h in (0, 4, 2, 6, 1, 5, 3, 7)]
    x = [v + pltpu.roll(v, 4, axis=0) for v in x]
    x = [jnp.where(sub < 4, x[2 * i], x[2 * i + 1]) for i in range(4)]
    x = [v + pltpu.roll(v, 6, axis=0) for v in x]
    x = [jnp.where((sub & 2) == 0, x[2 * i], pltpu.roll(x[2 * i + 1], 2, axis=0)) for i in range(2)]
    x = [v + pltpu.roll(v, 7, axis=0) for v in x]
    return jnp.where((sub & 1) == 0, x[0], pltpu.roll(x[1], 1, axis=0))


def _head_scores(prod):
    parts = []
    for hd in range(N_HEADS):
        part = prod(hd, 0)
        for r in range(1, HEAD_DIM // 8):
            part = part + prod(hd, r)
        parts.append(part)
    return _sum_heads(parts)


def _decode_begin(dec, seq):
    def column(ref):
        x = ref[...].astype(F32)
        lane = lax.broadcasted_iota(jnp.int32, x.shape, 1)
        return jnp.sum(jnp.where(lane == seq, x, 0.0), axis=1, keepdims=True)

    dec.q[...] = jnp.broadcast_to(column(dec.q_t), dec.q.shape)
    dec.kn[...] = column(dec.kn_t)
    dec.vn[...] = column(dec.vn_t)
    dec.lfn[...] = column(dec.lfn_t)
    dec.m[...] = jnp.full_like(dec.m, NEG_INF)
    dec.l[...] = jnp.zeros_like(dec.l)
    dec.carry[...] = jnp.zeros_like(dec.carry)
    dec.acc[...] = jnp.zeros_like(dec.acc)


def _decode_pages(dec):
    g_pages = len(dec.k)
    n_rows = N_HEADS * g_pages
    lf = jnp.concatenate([r[...] for r in dec.lf], axis=0)
    terms = jnp.concatenate([x.astype(BF16) for x in _split3(lf)], axis=0)
    sums = jnp.dot(terms, dec.scan[...], preferred_element_type=F32)
    sums = sums[0:n_rows] + sums[n_rows:2 * n_rows] + sums[2 * n_rows:3 * n_rows]

    carry = dec.carry[...]
    s_pages = []
    for g in range(g_pages):
        c = sums[8 * g:8 * g + 8, 0:LANES] + carry
        carry = carry + sums[8 * g:8 * g + 8, LANES:2 * LANES]
        s = _head_scores(lambda hd, r, g=g: dec.k[g][hd, 8 * r:8 * r + 8] * dec.q[hd * HEAD_DIM + 8 * r:hd * HEAD_DIM + 8 * r + 8])
        s_pages.append(s - c * LOG2E)
    dec.carry[...] = carry

    m_prev = dec.m[...]
    m_new = m_prev
    for s in s_pages:
        m_new = jnp.maximum(m_new, s)
    alpha = jnp.exp2(m_prev - m_new)
    p_pages = [jnp.exp2(s - m_new) for s in s_pages]
    l_new = alpha * dec.l[...]
    for p in p_pages:
        l_new = l_new + p
    dec.l[...] = l_new
    dec.m[...] = m_new
    for hd in range(N_HEADS):
        acc = alpha[hd:hd + 1] * dec.acc[hd]
        for g in range(g_pages):
            acc = acc + p_pages[g][hd:hd + 1] * dec.v[g][hd]
        dec.acc[hd] = acc


def _decode_finish(dec, seq):
    kn = dec.kn[...]
    s_new = _head_scores(lambda hd, r: dec.q[hd * HEAD_DIM + 8 * r:hd * HEAD_DIM + 8 * r + 8]
                         * kn[hd * HEAD_DIM + 8 * r:hd * HEAD_DIM + 8 * r + 8])
    s_new = s_new - (dec.carry[...] + dec.lfn[...]) * LOG2E
    m_lane = dec.m[...]
    m_fin = jnp.maximum(jnp.max(m_lane, axis=1, keepdims=True), s_new)
    w = jnp.exp2(m_lane - m_fin)
    p_new = jnp.exp2(s_new - m_fin)
    l_fin = jnp.sum(dec.l[...] * w, axis=1, keepdims=True) + p_new
    outs = []
    for hd in range(N_HEADS):
        rows = slice(hd * HEAD_DIM, (hd + 1) * HEAD_DIM)
        past = jnp.sum(dec.acc[hd] * w[hd:hd + 1], axis=1, keepdims=True)
        outs.append((past + p_new[hd:hd + 1] * dec.vn[rows]) / l_fin[hd:hd + 1])
    out = jnp.concatenate(outs, axis=0)
    lane = lax.broadcasted_iota(jnp.int32, dec.out.shape, 1)
    dec.out[...] = jnp.where(lane == seq, out, dec.out[...])


def _attn_kernel(qi_tab, kj_tab, pt_ref, qT_ref, kr_ref, kb_ref, vT_ref, *refs, g_pages, groups, n_items):
    n_dec_in = 3 * g_pages + 5
    o_ref, od_ref, qbd_ref, va_ref, m_ref, acc_ref = refs[n_dec_in:n_dec_in + 6]
    dec = _Decode(refs[0:g_pages], refs[g_pages:2 * g_pages], refs[2 * g_pages:3 * g_pages],
                  *refs[3 * g_pages:n_dec_in], od_ref, *refs[n_dec_in + 6:])
    tq = o_ref.shape[0]
    tk = kr_ref.shape[0]
    pair = pl.program_id(1)
    t = pl.program_id(2)
    qi = qi_tab[t]
    kj = kj_tab[t]
    item = (pl.program_id(0) * pl.num_programs(1) + pair) * pl.num_programs(2) + t
    seq = lax.div(item, groups)
    grp = lax.rem(item, groups)
    active = item < n_items

    @pl.when(item == 0)
    def _():
        od_ref[...] = jnp.zeros_like(od_ref)

    @pl.when(jnp.logical_and(active, grp == 0))
    def _():
        _decode_begin(dec, seq)

    @pl.when(kj == 0)
    def _():
        q = qT_ref[...]
        top = lax.broadcasted_iota(jnp.int32, q.shape, 0) < HEAD_DIM
        zero = jnp.zeros_like(q)
        qbd_ref[0:LANES] = jnp.concatenate([jnp.where(top, q, zero), jnp.where(top, zero, q)], axis=1)
        row = lax.broadcasted_iota(jnp.int32, (LANES, 2 * tq), 0) - 6 * pair
        second = lax.broadcasted_iota(jnp.int32, (LANES, 2 * tq), 1) >= tq
        lo = jnp.where(second, 3, 0)
        qbd_ref[LANES:2 * LANES] = jnp.where(jnp.logical_and(row >= lo, row < lo + 3), 1.0, 0.0).astype(BF16)
        ones_row = lax.broadcasted_iota(jnp.int32, (16, tk), 0) == 0
        va_ref[LANES:LANES + 16] = jnp.where(ones_row, 1.0, 0.0).astype(BF16)
        m_ref[...] = jnp.full_like(m_ref, NEG_INF)
        acc_ref[...] = jnp.zeros_like(acc_ref)

    def step(on_diagonal):
        keys = jnp.concatenate([kr_ref[...], kb_ref[...]], axis=1)
        s = jnp.dot(keys, qbd_ref[...], preferred_element_type=F32)
        if on_diagonal:
            key_pos = lax.broadcasted_iota(jnp.int32, (tk, tq), 0)
            qry_pos = lax.broadcasted_iota(jnp.int32, (tk, tq), 1)
            allowed = key_pos <= qry_pos
            s = jnp.where(jnp.concatenate([allowed, allowed], axis=1), s, NEG_INF)
        m_prev = m_ref[0:1]
        m_new = jnp.maximum(m_prev, jnp.max(s, axis=0, keepdims=True))
        alpha = jnp.exp2(m_prev - m_new)
        p = jnp.exp2(s - m_new).astype(BF16)
        va_ref[0:LANES] = vT_ref[...].astype(BF16)
        acc_ref[...] = alpha * acc_ref[...] + jnp.dot(va_ref[...], p, preferred_element_type=F32)
        m_ref[...] = jnp.broadcast_to(m_new, m_ref.shape)
        _decode_pages(dec)

    @pl.when(kj < qi)
    def _():
        step(False)

    @pl.when(kj == qi)
    def _():
        step(True)
        first = acc_ref[0:HEAD_DIM, 0:tq] / acc_ref[LANES:LANES + 1, 0:tq]
        second = acc_ref[HEAD_DIM:LANES, tq:2 * tq] / acc_ref[LANES:LANES + 1, tq:2 * tq]
        o_ref[...] = jnp.concatenate([first, second], axis=0).T.astype(o_ref.dtype)

    @pl.when(jnp.logical_and(active, grp == groups - 1))
    def _():
        _decode_finish(dec, seq)


def _attention(q_t, k_rows, k_bias, v_t, tile, page_table, cache_k_t, cache_v_t, cache_lf_t,
               qd_t, knd_t, vnd_t, lfnd_t, layer):
    b, _, s = q_t.shape
    n = s // tile
    qi_tab = jnp.asarray([i for i in range(n) for _ in range(i + 1)], jnp.int32)
    kj_tab = jnp.asarray([j for i in range(n) for j in range(i + 1)], jnp.int32)
    pairs = N_HEADS // 2
    tri = n * (n + 1) // 2
    steps = b * pairs * tri
    n_seq, n_pages = page_table.shape
    g_pages = min(g for g in range(1, n_pages + 1) if n_pages % g == 0 and n_seq * (n_pages // g) <= steps)
    groups = n_pages // g_pages
    n_items = n_seq * groups

    def item_page(g):
        return lambda i, p, t, qi, kj, pt: pt[jnp.minimum((i * pairs + p) * tri + t, n_items - 1) * g_pages + g]

    def kv_page(g):
        return pl.BlockSpec((None, None, N_HEADS, HEAD_DIM, LANES), lambda *a: (layer, item_page(g)(*a), 0, 0, 0))

    def lf_page(g):
        return pl.BlockSpec((None, None, N_HEADS, LANES), lambda *a: (layer, item_page(g)(*a), 0, 0))

    whole = lambda rows, cols=LANES: pl.BlockSpec((rows, cols), lambda *a: (0, 0))
    lane_ge_row = lax.broadcasted_iota(jnp.int32, (LANES, LANES), 1) >= lax.broadcasted_iota(jnp.int32, (LANES, LANES), 0)
    scan = jnp.concatenate([lane_ge_row.astype(BF16), jnp.ones((LANES, LANES), BF16)], axis=1)
    grid_spec = pltpu.PrefetchScalarGridSpec(
        num_scalar_prefetch=3,
        grid=(b, pairs, tri),
        in_specs=([
            pl.BlockSpec((None, LANES, tile), lambda i, p, t, qi, kj, pt: (i, p, qi[t])),
            pl.BlockSpec((None, tile, LANES), lambda i, p, t, qi, kj, pt: (i, kj[t], p)),
            pl.BlockSpec((None, tile, LANES), lambda i, p, t, qi, kj, pt: (i, kj[t], 0)),
            pl.BlockSpec((None, LANES, tile), lambda i, p, t, qi, kj, pt: (i, p, kj[t])),
        ] + [kv_page(g) for g in range(g_pages)] + [kv_page(g) for g in range(g_pages)]
          + [lf_page(g) for g in range(g_pages)]
          + [whole(ATT_WIDTH), whole(ATT_WIDTH), whole(ATT_WIDTH), whole(N_HEADS), whole(LANES, 2 * LANES)]),
        out_specs=[pl.BlockSpec((None, tile, LANES), lambda i, p, t, qi, kj, pt: (i, qi[t], p)), whole(ATT_WIDTH)],
        scratch_shapes=[
            pltpu.VMEM((2 * LANES, 2 * tile), BF16),
            pltpu.VMEM((LANES + 16, tile), BF16),
            pltpu.VMEM((8, 2 * tile), F32),
            pltpu.VMEM((LANES + 16, 2 * tile), F32),
            pltpu.VMEM((ATT_WIDTH, LANES), F32),
            pltpu.VMEM((ATT_WIDTH, 1), F32),
            pltpu.VMEM((ATT_WIDTH, 1), F32),
            pltpu.VMEM((N_HEADS, 1), F32),
            pltpu.VMEM((N_HEADS, LANES), F32),
            pltpu.VMEM((N_HEADS, LANES), F32),
            pltpu.VMEM((N_HEADS, LANES), F32),
            pltpu.VMEM((N_HEADS, HEAD_DIM, LANES), F32),
        ],
    )
    return pl.pallas_call(
        functools.partial(_attn_kernel, g_pages=g_pages, groups=groups, n_items=n_items),
        grid_spec=grid_spec,
        out_shape=[jax.ShapeDtypeStruct((b, s, ATT_WIDTH), BF16), jax.ShapeDtypeStruct((ATT_WIDTH, LANES), F32)],
        compiler_params=pltpu.CompilerParams(dimension_semantics=("arbitrary", "arbitrary", "arbitrary"),
                                             vmem_limit_bytes=VMEM_LIMIT),
        name="attention",
    )(qi_tab, kj_tab, page_table.reshape(-1), q_t, k_rows, k_bias, v_t,
      *([cache_k_t] * g_pages), *([cache_v_t] * g_pages), *([cache_lf_t] * g_pages),
      qd_t, knd_t, vnd_t, lfnd_t, scan)


def _mix_tail(d_groups, attn, gates, x, wp_ref, ps_ref, wua_ref, wup_ref, wo_ref):
    pooled = []
    for g in range(len(POOL_WINDOWS)):
        y = jnp.dot(d_groups[g].astype(BF16), wp_ref[g], preferred_element_type=F32)
        pooled.append((y * ps_ref[:, g * POOL_GROUP:(g + 1) * POOL_GROUP]).astype(BF16))
    pool = jnp.concatenate(pooled, axis=1)
    d_model = x.shape[1]
    a = jnp.dot(attn, wua_ref[...], preferred_element_type=F32)
    b = jnp.dot(pool, wup_ref[...], preferred_element_type=F32)
    gates = gates.astype(F32)
    mixed = jax.nn.sigmoid(gates[:, 0:d_model]) * a + jax.nn.sigmoid(gates[:, d_model:2 * d_model]) * b
    return x + jnp.dot(mixed.astype(BF16), wo_ref[...], preferred_element_type=F32)


def _mix_prompt_kernel(attn_ref, u_ref, g_ref, x_ref, wp_ref, ps_ref, wua_ref, wup_ref, wo_ref, o_ref, ext_ref):
    tm = u_ref.shape[0]
    j = pl.program_id(1)

    @pl.when(j == 0)
    def _():
        ext_ref[0:HALO] = jnp.zeros((HALO, POOL_WIDTH), F32)

    @pl.when(j > 0)
    def _():
        ext_ref[0:HALO] = ext_ref[tm:tm + HALO]

    ext_ref[HALO:HALO + tm] = u_ref[...]
    pos = j * tm + lax.broadcasted_iota(jnp.int32, (tm, 1), 0)
    d_groups = []
    for g, w in enumerate(POOL_WINDOWS):
        lanes = slice(g * POOL_GROUP, (g + 1) * POOL_GROUP)
        win = ext_ref[HALO:HALO + tm, lanes]
        for back in range(1, w):
            win = win + ext_ref[HALO - back:HALO - back + tm, lanes]
        cnt = jnp.minimum(pos + 1, w).astype(F32)
        d_groups.append(win / cnt - ext_ref[HALO:HALO + tm, lanes])
    o_ref[...] = _mix_tail(d_groups, attn_ref[...], g_ref[...], x_ref[...], wp_ref, ps_ref, wua_ref, wup_ref, wo_ref)


def _mix_decode_kernel(attn_ref, u_ref, g_ref, x_ref, st_ref, wp_ref, ps_ref, wua_ref, wup_ref, wo_ref, o_ref):
    tm = u_ref.shape[0]
    n_seq = st_ref.shape[1]
    d_groups = []
    for g, w in enumerate(POOL_WINDOWS):
        lanes = slice(g * POOL_GROUP, (g + 1) * POOL_GROUP)
        u_new = u_ref[0:n_seq, lanes]
        win = u_new
        for back in range(1, w):
            win = win + st_ref[POOL_CTX - back, :, lanes]
        d = win / float(w) - u_new
        d_groups.append(jnp.concatenate([d, jnp.zeros((tm - n_seq, POOL_GROUP), F32)], axis=0))
    o_ref[...] = _mix_tail(d_groups, attn_ref[...], g_ref[...], x_ref[...], wp_ref, ps_ref, wua_ref, wup_ref, wo_ref)


def _mix_weight_specs(w, layer, d):
    lay3 = lambda *_: (layer, 0, 0)
    lay4 = lambda *_: (layer, 0, 0, 0)
    specs = [
        _const_spec((None, len(POOL_WINDOWS), POOL_GROUP, POOL_GROUP), lay4),
        _const_spec((None, 1, POOL_WIDTH), lay3),
        _const_spec((None, ATT_WIDTH, d), lay3),
        _const_spec((None, POOL_WIDTH, d), lay3),
        _const_spec((None, d, d), lay3),
    ]
    return specs, (w["w_pool_grp"], w["pool_scale"], w["w_up_attn"], w["w_up_pool"], w["w_out"])


def _mix_prompt(attn, u, g, x, w, layer, tm):
    b, s, d = x.shape
    tok = lambda cols: pl.BlockSpec((None, tm, cols), lambda i, j: (i, j, 0))
    w_specs, w_args = _mix_weight_specs(w, layer, d)
    return pl.pallas_call(
        _mix_prompt_kernel,
        grid=(b, s // tm),
        in_specs=[tok(ATT_WIDTH), tok(POOL_WIDTH), tok(2 * d), tok(d)] + w_specs,
        out_specs=tok(d),
        out_shape=jax.ShapeDtypeStruct((b, s, d), F32),
        scratch_shapes=[pltpu.VMEM((HALO + tm, POOL_WIDTH), F32)],
        compiler_params=pltpu.CompilerParams(dimension_semantics=("parallel", "arbitrary"),
                                             vmem_limit_bytes=VMEM_LIMIT),
        name="mix_prompt",
    )(attn, u, g, x, *w_args)


def _mix_decode(attn, u, g, x, state_t, w, layer):
    rows, d = x.shape
    n_seq = state_t.shape[2]
    full = lambda cols: pl.BlockSpec((rows, cols), lambda i: (0, 0))
    w_specs, w_args = _mix_weight_specs(w, layer, d)
    return pl.pallas_call(
        _mix_decode_kernel,
        grid=(1,),
        in_specs=[full(ATT_WIDTH), full(POOL_WIDTH), full(2 * d), full(d),
                  pl.BlockSpec((None, POOL_CTX, n_seq, POOL_WIDTH), lambda i: (layer, 0, 0, 0))] + w_specs,
        out_specs=full(d),
        out_shape=jax.ShapeDtypeStruct((rows, d), F32),
        compiler_params=pltpu.CompilerParams(dimension_semantics=("arbitrary",), vmem_limit_bytes=VMEM_LIMIT),
        name="mix_decode",
    )(attn, u, g, x, state_t, *w_args)


def _ffn_ple_kernel(x_ref, p_ref, nf_ref, wgate_ref, wup_ref, wdown_ref, np_ref, wpg_ref, wpp_ref, o_ref):
    x = x_ref[...]
    h = _rms_rows(x, nf_ref[...]).astype(BF16)
    hidden = wgate_ref.shape[1]
    chunk = hidden // FFN_CHUNKS
    y = x
    for c in range(FFN_CHUNKS):
        cols = slice(c * chunk, (c + 1) * chunk)
        gate = jnp.dot(h, wgate_ref[:, cols], preferred_element_type=F32)
        up = jnp.dot(h, wup_ref[:, cols], preferred_element_type=F32)
        act = (jax.nn.silu(gate) * up).astype(BF16)
        y = y + jnp.dot(act, wdown_ref[cols, :], preferred_element_type=F32)
    h2 = _rms_rows(y, np_ref[...]).astype(BF16)
    gate2 = jax.nn.sigmoid(jnp.dot(h2, wpg_ref[...], preferred_element_type=F32))
    emb = jnp.dot(p_ref[...].astype(BF16), wpp_ref[...], preferred_element_type=F32)
    o_ref[...] = y + gate2 * emb


def _ffn_ple(x, p_all, w, layer, tm):
    m, d = x.shape
    ple = p_all.shape[2]
    hidden = w["w_down"].shape[1]
    lay = lambda *_: (layer, 0, 0)
    return pl.pallas_call(
        _ffn_ple_kernel,
        grid=(m // tm,),
        in_specs=[
            pl.BlockSpec((tm, d), lambda i: (i, 0)),
            pl.BlockSpec((None, tm, ple), lambda i: (layer, i, 0)),
            _const_spec((None, 1, d), lay),
            _const_spec((None, d, hidden), lay),
            _const_spec((None, d, hidden), lambda *_: (layer, 0, 1)),
            _const_spec((None, hidden, d), lay),
            _const_spec((None, 1, d), lay),
            _const_spec((None, d, d), lay),
            _const_spec((None, ple, d), lay),
        ],
        out_specs=pl.BlockSpec((tm, d), lambda i: (i, 0)),
        out_shape=jax.ShapeDtypeStruct((m, d), F32),
        compiler_params=pltpu.CompilerParams(dimension_semantics=("parallel",), vmem_limit_bytes=VMEM_LIMIT),
        name="ffn_ple",
    )(x, p_all, w["norm_ffn"], w["w_gate_up"], w["w_gate_up"], w["w_down"], w["norm_ple"], w["w_ple_gate"], w["w_ple_proj"])


def _prepare_weights(norm_mix, w_in, b_forget, q_gain, k_gain, w_pool_grp, pool_scale, w_up_attn, w_up_pool,
                     w_out, norm_ffn, w_gate_up, w_down, norm_ple, w_ple_gate, w_ple_proj):
    a = ATT_WIDTH
    f0, u0, g0 = 3 * a, 3 * a + N_HEADS, 3 * a + N_HEADS + POOL_WIDTH
    w_in_t = jnp.swapaxes(w_in, 1, 2)
    wf_t = jnp.pad(w_in_t[:, f0:u0], ((0, 0), (0, 16 - N_HEADS), (0, 0)))
    return {
        "norm_mix": norm_mix[:, None, :],
        "wqkv_t": w_in_t[:, :f0].astype(BF16),
        "wf_t": wf_t.astype(BF16),
        "w_u": w_in[:, :, u0:g0].astype(BF16),
        "w_g": w_in[:, :, g0:].astype(BF16),
        "b_forget": b_forget[:, :, None],
        "q_gain": jnp.tile(q_gain, (1, N_HEADS))[:, :, None],
        "k_gain": jnp.tile(k_gain, (1, N_HEADS))[:, :, None],
        "w_pool_grp": w_pool_grp.astype(BF16),
        "pool_scale": pool_scale[:, None, :],
        "w_up_attn": w_up_attn.astype(BF16),
        "w_up_pool": w_up_pool.astype(BF16),
        "w_out": w_out.astype(BF16),
        "norm_ffn": norm_ffn[:, None, :],
        "w_gate_up": w_gate_up.astype(BF16),
        "w_down": w_down.astype(BF16),
        "norm_ple": norm_ple[:, None, :],
        "w_ple_gate": w_ple_gate.astype(BF16),
        "w_ple_proj": w_ple_proj.astype(BF16),
    }


def _tile(n, preferred):
    return preferred if n % preferred == 0 else n


def kernel(x_prompt, x_sample, cache_k, cache_v, cache_logf, state_pool, page_table, p_prompt, p_sample, norm_mix, w_in, b_forget, q_gain, k_gain, w_pool_grp, pool_scale, w_up_attn, w_up_pool, w_out, norm_ffn, w_gate_up, w_down, norm_ple, w_ple_gate, w_ple_proj):
    depth = w_in.shape[0]
    b, s, d = x_prompt.shape
    n_seq = x_sample.shape[0]
    ple = p_prompt.shape[-1]
    assert x_sample.shape[1] == 1 and n_seq <= DEC_ROWS and s % LANES == 0
    w = _prepare_weights(norm_mix, w_in, b_forget, q_gain, k_gain, w_pool_grp, pool_scale, w_up_attn, w_up_pool,
                         w_out, norm_ffn, w_gate_up, w_down, norm_ple, w_ple_gate, w_ple_proj)

    cache_k_t = jnp.transpose(cache_k, (0, 1, 3, 4, 2))
    cache_v_t = jnp.transpose(cache_v, (0, 1, 3, 4, 2))
    cache_lf_t = jnp.transpose(cache_logf, (0, 1, 3, 2))
    state_t = jnp.transpose(state_pool, (0, 2, 1, 3))

    tm = _tile(s, 512)
    pad_rows = DEC_ROWS - n_seq
    xp = x_prompt
    xs = jnp.pad(x_sample.reshape(n_seq, d), ((0, pad_rows), (0, 0)))
    pp = p_prompt.reshape(depth, b * s, ple)
    ps = jnp.pad(p_sample.reshape(depth, n_seq, ple), ((0, 0), (0, pad_rows), (0, 0)))

    k_p, v_p, lf_p, pool_p, k_s, v_s, lf_s, pool_s = [], [], [], [], [], [], [], []
    for i in range(depth):
        q_t, k_t, v_t, lf_t, u, g, k_rows, k_bias = _inproj(xp, w, i, tm)
        qd_t, kd_t, vd_t, lfd_t, ud, gd, _, _ = _inproj(xs[None], w, i, DEC_ROWS)
        attn, attn_dec_t = _attention(q_t, k_rows, k_bias, v_t, tm, page_table, cache_k_t, cache_v_t, cache_lf_t,
                                      qd_t[0], kd_t[0], vd_t[0], lfd_t[0], i)
        xp = _mix_prompt(attn, u, g, xp, w, i, tm)
        xp = _ffn_ple(xp.reshape(b * s, d), pp, w, i, tm).reshape(b, s, d)
        k_p.append(k_t)
        v_p.append(v_t)
        lf_p.append(lf_t)
        pool_p.append(u[:, s - POOL_CTX:])

        xs = _mix_decode(attn_dec_t.T.astype(BF16), ud[0], gd[0], xs, state_t, w, i)
        xs = _ffn_ple(xs, ps, w, i, DEC_ROWS)
        k_s.append(kd_t[0, :, :n_seq])
        v_s.append(vd_t[0, :, :n_seq])
        lf_s.append(lfd_t[0, :, :n_seq])
        pool_s.append(jnp.concatenate([state_t[i, 1:], ud[0, None, :n_seq]], axis=0))

    def heads_last(xs_t, n_tok):
        x5 = jnp.stack(xs_t).reshape(depth, -1, N_HEADS, HEAD_DIM, n_tok)
        return jnp.transpose(x5, (0, 1, 4, 2, 3))

    k_prompt = heads_last(k_p, s)
    v_prompt = heads_last(v_p, s)
    logf_prompt = jnp.transpose(jnp.stack(lf_p), (0, 1, 3, 2))
    pool_prompt = jnp.stack(pool_p)
    k_sample = jnp.transpose(jnp.stack(k_s), (0, 2, 1)).reshape(depth, n_seq, 1, N_HEADS, HEAD_DIM)
    v_sample = jnp.transpose(jnp.stack(v_s), (0, 2, 1)).reshape(depth, n_seq, 1, N_HEADS, HEAD_DIM)
    logf_sample = jnp.transpose(jnp.stack(lf_s), (0, 2, 1))[:, :, None, :]
    pool_sample = jnp.transpose(jnp.stack(pool_s), (0, 2, 1, 3))
    y_sample = xs[:n_seq].reshape(n_seq, 1, d)
    return (xp, y_sample, k_prompt, v_prompt, logf_prompt, pool_prompt, k_sample, v_sample, logf_sample, pool_sample)
```

```python
import functools
from typing import Any, NamedTuple

import jax
import jax.numpy as jnp
from jax import lax
from jax.experimental import pallas as pl
from jax.experimental.pallas import tpu as pltpu

F32 = jnp.float32
BF16 = jnp.bfloat16

N_HEADS = 8
HEAD_DIM = 64
ATT_WIDTH = N_HEADS * HEAD_DIM
POOL_WINDOWS = (2, 4, 8, 16)
POOL_GROUP = 128
POOL_WIDTH = POOL_GROUP * len(POOL_WINDOWS)
POOL_CTX = max(POOL_WINDOWS) - 1
HALO = POOL_CTX + 1
RMS_EPS = 1e-6
NEG_INF = -1e30
LOG2E = 1.4426950408889634
QK_SCALE_LOG2 = HEAD_DIM ** -0.5 * LOG2E
BIAS_ROWS = 32
LANES = 128
DEC_ROWS = 128
FFN_CHUNKS = 2
QUERY_BLOCKS = 4
VMEM_LIMIT = 52 * 1024 * 1024

_NT = (((1,), (1,)), ((), ()))


def _const_spec(block, index):
    return pl.BlockSpec(block, index, pipeline_mode=pl.Buffered(1))


def _rms_rows(x, gain):
    return x * lax.rsqrt(jnp.mean(x * x, axis=-1, keepdims=True) + RMS_EPS) * gain


def _log_sigmoid(x):
    return -(jnp.maximum(-x, 0.0) + jnp.log1p(jnp.exp(-jnp.abs(x))))


def _cumsum_lanes(x):
    n = x.shape[-1]
    lane = lax.broadcasted_iota(jnp.int32, x.shape, x.ndim - 1)
    shift = 1
    while shift < n:
        x = x + jnp.where(lane >= shift, pltpu.roll(x, shift, axis=x.ndim - 1), 0.0)
        shift *= 2
    return x


def _split3(x):
    hi = x.astype(BF16).astype(F32)
    mid = (x - hi).astype(BF16).astype(F32)
    lo = (x - hi - mid).astype(BF16).astype(F32)
    return hi, mid, lo


def _inproj_kernel(x_ref, nrm_ref, wqkv_ref, wf_ref, wu_ref, wg_ref, bf_ref, qg_ref, kg_ref,
                   qT_ref, kT_ref, vT_ref, lfT_ref, u_ref, g_ref, kr_ref, kb_ref, carry_ref):
    tm = x_ref.shape[0]
    h = _rms_rows(x_ref[...], nrm_ref[...]).astype(BF16)

    def proj_t(rows):
        return lax.dot_general(rows, h, _NT, preferred_element_type=F32)

    def head_norm(y_t, gain_col):
        parts = []
        for hd in range(N_HEADS):
            y = y_t[hd * HEAD_DIM:(hd + 1) * HEAD_DIM]
            parts.append(y * lax.rsqrt(jnp.mean(y * y, axis=0, keepdims=True) + RMS_EPS))
        return jnp.concatenate(parts, axis=0) * gain_col

    q_t = head_norm(proj_t(wqkv_ref[0:ATT_WIDTH, :]), qg_ref[...])
    qT_ref[...] = (q_t * QK_SCALE_LOG2).astype(BF16)
    k_t = head_norm(proj_t(wqkv_ref[ATT_WIDTH:2 * ATT_WIDTH, :]), kg_ref[...])
    kT_ref[...] = k_t
    kr_ref[...] = k_t.T.astype(BF16)
    vT_ref[...] = proj_t(wqkv_ref[2 * ATT_WIDTH:3 * ATT_WIDTH, :])

    logf = _log_sigmoid(proj_t(wf_ref[...])[0:N_HEADS] + bf_ref[...])
    lfT_ref[...] = logf

    @pl.when(pl.program_id(1) == 0)
    def _():
        carry_ref[...] = jnp.zeros_like(carry_ref)

    c = _cumsum_lanes(logf) + carry_ref[:, 0:1]
    carry_ref[...] = jnp.broadcast_to(c[:, tm - 1:tm], carry_ref.shape)

    terms = _split3(c * (-LOG2E))
    row = lax.broadcasted_iota(jnp.int32, (BIAS_ROWS, tm), 0)
    bias_t = jnp.zeros((BIAS_ROWS, tm), F32)
    for hd in range(N_HEADS):
        for term in range(3):
            bias_t = jnp.where(row == 3 * hd + term, terms[term][hd:hd + 1], bias_t)
    bias_t = jnp.concatenate([bias_t, jnp.zeros((LANES - BIAS_ROWS, tm), F32)], axis=0)
    kb_ref[...] = bias_t.T.astype(BF16)

    u_ref[...] = jnp.dot(h, wu_ref[...], preferred_element_type=F32)
    g_ref[...] = jnp.dot(h, wg_ref[...], preferred_element_type=F32).astype(g_ref.dtype)


def _inproj(x, w, layer, tm):
    b, s, d = x.shape
    lay = lambda *_: (layer, 0, 0)
    tok_t = lambda rows: pl.BlockSpec((None, rows, tm), lambda i, j: (i, 0, j))
    tok = lambda cols: pl.BlockSpec((None, tm, cols), lambda i, j: (i, j, 0))
    return pl.pallas_call(
        _inproj_kernel,
        grid=(b, s // tm),
        in_specs=[
            tok(d),
            _const_spec((None, 1, d), lay),
            _const_spec((None, 3 * ATT_WIDTH, d), lay),
            _const_spec((None, 16, d), lay),
            _const_spec((None, d, POOL_WIDTH), lay),
            _const_spec((None, d, 2 * d), lay),
            _const_spec((None, N_HEADS, 1), lay),
            _const_spec((None, ATT_WIDTH, 1), lay),
            _const_spec((None, ATT_WIDTH, 1), lay),
        ],
        out_specs=[tok_t(ATT_WIDTH), tok_t(ATT_WIDTH), tok_t(ATT_WIDTH), tok_t(N_HEADS),
                   tok(POOL_WIDTH), tok(2 * d), tok(ATT_WIDTH), tok(LANES)],
        out_shape=[
            jax.ShapeDtypeStruct((b, ATT_WIDTH, s), BF16),
            jax.ShapeDtypeStruct((b, ATT_WIDTH, s), F32),
            jax.ShapeDtypeStruct((b, ATT_WIDTH, s), F32),
            jax.ShapeDtypeStruct((b, N_HEADS, s), F32),
            jax.ShapeDtypeStruct((b, s, POOL_WIDTH), F32),
            jax.ShapeDtypeStruct((b, s, 2 * d), BF16),
            jax.ShapeDtypeStruct((b, s, ATT_WIDTH), BF16),
            jax.ShapeDtypeStruct((b, s, LANES), BF16),
        ],
        scratch_shapes=[pltpu.VMEM((N_HEADS, LANES), F32)],
        compiler_params=pltpu.CompilerParams(dimension_semantics=("parallel", "arbitrary"),
                                             vmem_limit_bytes=VMEM_LIMIT),
        name="inproj",
    )(x, w["norm_mix"], w["wqkv_t"], w["wf_t"], w["w_u"], w["w_g"], w["b_forget"], w["q_gain"], w["k_gain"])


class _Decode(NamedTuple):
    k: Any
    v: Any
    lf: Any
    q_t: Any
    kn_t: Any
    vn_t: Any
    lfn_t: Any
    scan: Any
    out: Any
    q: Any
    kn: Any
    vn: Any
    lfn: Any
    m: Any
    l: Any
    carry: Any
    acc: Any


def _sum_heads(parts):
    sub = lax.broadcasted_iota(jnp.int32, parts[0].shape, 0)
    x = [parts[h] for h in (0, 4, 2, 6, 1, 5, 3, 7)]
    x = [v + pltpu.roll(v, 4, axis=0) for v in x]
    x = [jnp.where(sub < 4, x[2 * i], x[2 * i + 1]) for i in range(4)]
    x = [v + pltpu.roll(v, 6, axis=0) for v in x]
    x = [jnp.where((sub & 2) == 0, x[2 * i], pltpu.roll(x[2 * i + 1], 2, axis=0)) for i in range(2)]
    x = [v + pltpu.roll(v, 7, axis=0) for v in x]
    return jnp.where((sub & 1) == 0, x[0], pltpu.roll(x[1], 1, axis=0))


def _head_scores(prod):
    parts = []
    for hd in range(N_HEADS):
        part = prod(hd, 0)
        for r in range(1, HEAD_DIM // 8):
            part = part + prod(hd, r)
        parts.append(part)
    return _sum_heads(parts)


def _decode_begin(dec, seq):
    def column(ref):
        x = ref[...].astype(F32)
        lane = lax.broadcasted_iota(jnp.int32, x.shape, 1)
        return jnp.sum(jnp.where(lane == seq, x, 0.0), axis=1, keepdims=True)

    dec.q[...] = jnp.broadcast_to(column(dec.q_t), dec.q.shape)
    dec.kn[...] = column(dec.kn_t)
    dec.vn[...] = column(dec.vn_t)
    dec.lfn[...] = column(dec.lfn_t)
    dec.m[...] = jnp.full_like(dec.m, NEG_INF)
    dec.l[...] = jnp.zeros_like(dec.l)
    dec.carry[...] = jnp.zeros_like(dec.carry)
    dec.acc[...] = jnp.zeros_like(dec.acc)


def _decode_pages(dec):
    g_pages = len(dec.k)
    n_rows = N_HEADS * g_pages
    lf = jnp.concatenate([r[...] for r in dec.lf], axis=0)
    terms = jnp.concatenate([x.astype(BF16) for x in _split3(lf)], axis=0)
    sums = jnp.dot(terms, dec.scan[...], preferred_element_type=F32)
    sums = sums[0:n_rows] + sums[n_rows:2 * n_rows] + sums[2 * n_rows:3 * n_rows]

    carry = dec.carry[...]
    s_pages = []
    for g in range(g_pages):
        c = sums[8 * g:8 * g + 8, 0:LANES] + carry
        carry = carry + sums[8 * g:8 * g + 8, LANES:2 * LANES]
        s = _head_scores(lambda hd, r, g=g: dec.k[g][hd, 8 * r:8 * r + 8] * dec.q[hd * HEAD_DIM + 8 * r:hd * HEAD_DIM + 8 * r + 8])
        s_pages.append(s - c * LOG2E)
    dec.carry[...] = carry

    m_prev = dec.m[...]
    m_new = m_prev
    for s in s_pages:
        m_new = jnp.maximum(m_new, s)
    alpha = jnp.exp2(m_prev - m_new)
    p_pages = [jnp.exp2(s - m_new) for s in s_pages]
    l_new = alpha * dec.l[...]
    for p in p_pages:
        l_new = l_new + p
    dec.l[...] = l_new
    dec.m[...] = m_new
    for hd in range(N_HEADS):
        acc = alpha[hd:hd + 1] * dec.acc[hd]
        for g in range(g_pages):
            acc = acc + p_pages[g][hd:hd + 1] * dec.v[g][hd]
        dec.acc[hd] = acc


def _decode_finish(dec, seq):
    kn = dec.kn[...]
    s_new = _head_scores(lambda hd, r: dec.q[hd * HEAD_DIM + 8 * r:hd * HEAD_DIM + 8 * r + 8]
                         * kn[hd * HEAD_DIM + 8 * r:hd * HEAD_DIM + 8 * r + 8])
    s_new = s_new - (dec.carry[...] + dec.lfn[...]) * LOG2E
    m_lane = dec.m[...]
    m_fin = jnp.maximum(jnp.max(m_lane, axis=1, keepdims=True), s_new)
    w = jnp.exp2(m_lane - m_fin)
    p_new = jnp.exp2(s_new - m_fin)
    l_fin = jnp.sum(dec.l[...] * w, axis=1, keepdims=True) + p_new
    outs = []
    for hd in range(N_HEADS):
        rows = slice(hd * HEAD_DIM, (hd + 1) * HEAD_DIM)
        past = jnp.sum(dec.acc[hd] * w[hd:hd + 1], axis=1, keepdims=True)
        outs.append((past + p_new[hd:hd + 1] * dec.vn[rows]) / l_fin[hd:hd + 1])
    out = jnp.concatenate(outs, axis=0)
    lane = lax.broadcasted_iota(jnp.int32, dec.out.shape, 1)
    dec.out[...] = jnp.where(lane == seq, out, dec.out[...])


def _attn_kernel(b_tab, pair_tab, qi_tab, kj_tab, page_tab, qT_ref, kr_ref, kb_ref, vT_ref, *refs,
                 g_pages, groups, n_items):
    n_dec_in = 3 * g_pages + 5
    o_ref, od_ref, qbd_ref, va_ref, m_ref, acc_ref = refs[n_dec_in:n_dec_in + 6]
    dec = _Decode(refs[0:g_pages], refs[g_pages:2 * g_pages], refs[2 * g_pages:3 * g_pages],
                  *refs[3 * g_pages:n_dec_in], od_ref, *refs[n_dec_in + 6:])
    tq = o_ref.shape[0]
    tk = kr_ref.shape[0]
    item = pl.program_id(0)
    pair = pair_tab[item]
    qi = qi_tab[item]
    kj = kj_tab[item]
    seq = lax.div(item, groups)
    grp = lax.rem(item, groups)
    active = item < n_items

    @pl.when(item == 0)
    def _():
        od_ref[...] = jnp.zeros_like(od_ref)

    @pl.when(jnp.logical_and(active, grp == 0))
    def _():
        _decode_begin(dec, seq)

    @pl.when(kj == 0)
    def _():
        q = qT_ref[...]
        top = lax.broadcasted_iota(jnp.int32, q.shape, 0) < HEAD_DIM
        zero = jnp.zeros_like(q)
        qbd_ref[0:LANES] = jnp.concatenate([jnp.where(top, q, zero), jnp.where(top, zero, q)], axis=1)
        row = lax.broadcasted_iota(jnp.int32, (LANES, 2 * tq), 0) - 6 * pair
        second = lax.broadcasted_iota(jnp.int32, (LANES, 2 * tq), 1) >= tq
        lo = jnp.where(second, 3, 0)
        qbd_ref[LANES:2 * LANES] = jnp.where(jnp.logical_and(row >= lo, row < lo + 3), 1.0, 0.0).astype(BF16)
        ones_row = lax.broadcasted_iota(jnp.int32, (16, tk), 0) == 0
        va_ref[LANES:LANES + 16] = jnp.where(ones_row, 1.0, 0.0).astype(BF16)
        m_ref[...] = jnp.full_like(m_ref, NEG_INF)
        acc_ref[...] = jnp.zeros_like(acc_ref)

    def step(on_diagonal):
        keys = jnp.concatenate([kr_ref[...], kb_ref[...]], axis=1)
        va_ref[0:LANES] = vT_ref[...].astype(BF16)
        s_all = jnp.dot(keys, qbd_ref[...], preferred_element_type=F32)
        width = 2 * tq // QUERY_BLOCKS
        for blk in range(QUERY_BLOCKS):
            cols = slice(blk * width, (blk + 1) * width)
            first_qry = (blk * width) % tq
            n_keys = min(tk, first_qry + width) if on_diagonal else tk
            s = s_all[0:n_keys, cols]
            if on_diagonal:
                key_pos = lax.broadcasted_iota(jnp.int32, (n_keys, width), 0)
                qry_pos = first_qry + lax.broadcasted_iota(jnp.int32, (n_keys, width), 1)
                s = jnp.where(key_pos <= qry_pos, s, NEG_INF)
            m_prev = m_ref[0:1, cols]
            m_new = jnp.maximum(m_prev, jnp.max(s, axis=0, keepdims=True))
            alpha = jnp.exp2(m_prev - m_new)
            p = jnp.exp2(s - m_new).astype(BF16)
            pv = jnp.dot(va_ref[:, 0:n_keys], p, preferred_element_type=F32)
            acc_ref[:, cols] = alpha * acc_ref[:, cols] + pv
            m_ref[:, cols] = jnp.broadcast_to(m_new, (m_ref.shape[0], width))
        _decode_pages(dec)

    @pl.when(kj < qi)
    def _():
        step(False)

    @pl.when(kj == qi)
    def _():
        step(True)
        first = acc_ref[0:HEAD_DIM, 0:tq] / acc_ref[LANES:LANES + 1, 0:tq]
        second = acc_ref[HEAD_DIM:LANES, tq:2 * tq] / acc_ref[LANES:LANES + 1, tq:2 * tq]
        o_ref[...] = jnp.concatenate([first, second], axis=0).T.astype(o_ref.dtype)

    @pl.when(jnp.logical_and(active, grp == groups - 1))
    def _():
        _decode_finish(dec, seq)


def _attention(q_t, k_rows, k_bias, v_t, tile, page_table, cache_k_t, cache_v_t, cache_lf_t,
               qd_t, knd_t, vnd_t, lfnd_t, layer):
    b, _, s = q_t.shape
    n = s // tile
    pairs = N_HEADS // 2
    tiles = [(i, j) for i in range(n) for j in range(i + 1)]
    steps = b * pairs * len(tiles)
    n_seq, n_pages = page_table.shape
    g_pages = min(g for g in range(1, n_pages + 1) if n_pages % g == 0 and n_seq * (n_pages // g) <= steps)
    groups = n_pages // g_pages
    n_items = n_seq * groups
    b_tab = jnp.asarray([i for i in range(b) for _ in range(pairs * len(tiles))], jnp.int32)
    pair_tab = jnp.asarray([p for _ in range(b) for p in range(pairs) for _ in tiles], jnp.int32)
    qi_tab = jnp.asarray([t[0] for _ in range(b * pairs) for t in tiles], jnp.int32)
    kj_tab = jnp.asarray([t[1] for _ in range(b * pairs) for t in tiles], jnp.int32)
    item_of_step = jnp.minimum(jnp.arange(steps), n_items - 1)
    page_tab = page_table.reshape(n_items, g_pages)[item_of_step].reshape(-1)

    def kv_page(g):
        return pl.BlockSpec((None, None, N_HEADS, HEAD_DIM, LANES),
                            lambda s, bt, pt, qi, kj, pg: (layer, pg[s * g_pages + g], 0, 0, 0))

    def lf_page(g):
        return pl.BlockSpec((None, None, N_HEADS, LANES), lambda s, bt, pt, qi, kj, pg: (layer, pg[s * g_pages + g], 0, 0))

    whole = lambda rows, cols=LANES: pl.BlockSpec((rows, cols), lambda *a: (0, 0))
    lane_ge_row = lax.broadcasted_iota(jnp.int32, (LANES, LANES), 1) >= lax.broadcasted_iota(jnp.int32, (LANES, LANES), 0)
    scan = jnp.concatenate([lane_ge_row.astype(BF16), jnp.ones((LANES, LANES), BF16)], axis=1)
    grid_spec = pltpu.PrefetchScalarGridSpec(
        num_scalar_prefetch=5,
        grid=(steps,),
        in_specs=([
            pl.BlockSpec((None, LANES, tile), lambda s, bt, pt, qi, kj, pg: (bt[s], pt[s], qi[s])),
            pl.BlockSpec((None, tile, LANES), lambda s, bt, pt, qi, kj, pg: (bt[s], kj[s], pt[s])),
            pl.BlockSpec((None, tile, LANES), lambda s, bt, pt, qi, kj, pg: (bt[s], kj[s], 0)),
            pl.BlockSpec((None, LANES, tile), lambda s, bt, pt, qi, kj, pg: (bt[s], pt[s], kj[s])),
        ] + [kv_page(g) for g in range(g_pages)] + [kv_page(g) for g in range(g_pages)]
          + [lf_page(g) for g in range(g_pages)]
          + [whole(ATT_WIDTH), whole(ATT_WIDTH), whole(ATT_WIDTH), whole(N_HEADS), whole(LANES, 2 * LANES)]),
        out_specs=[pl.BlockSpec((None, tile, LANES), lambda s, bt, pt, qi, kj, pg: (bt[s], qi[s], pt[s])),
                   whole(ATT_WIDTH)],
        scratch_shapes=[
            pltpu.VMEM((2 * LANES, 2 * tile), BF16),
            pltpu.VMEM((LANES + 16, tile), BF16),
            pltpu.VMEM((8, 2 * tile), F32),
            pltpu.VMEM((LANES + 16, 2 * tile), F32),
            pltpu.VMEM((ATT_WIDTH, LANES), F32),
            pltpu.VMEM((ATT_WIDTH, 1), F32),
            pltpu.VMEM((ATT_WIDTH, 1), F32),
            pltpu.VMEM((N_HEADS, 1), F32),
            pltpu.VMEM((N_HEADS, LANES), F32),
            pltpu.VMEM((N_HEADS, LANES), F32),
            pltpu.VMEM((N_HEADS, LANES), F32),
            pltpu.VMEM((N_HEADS, HEAD_DIM, LANES), F32),
        ],
    )
    return pl.pallas_call(
        functools.partial(_attn_kernel, g_pages=g_pages, groups=groups, n_items=n_items),
        grid_spec=grid_spec,
        out_shape=[jax.ShapeDtypeStruct((b, s, ATT_WIDTH), BF16), jax.ShapeDtypeStruct((ATT_WIDTH, LANES), F32)],
        compiler_params=pltpu.CompilerParams(dimension_semantics=("arbitrary",), vmem_limit_bytes=VMEM_LIMIT),
        name="attention",
    )(b_tab, pair_tab, qi_tab, kj_tab, page_tab, q_t, k_rows, k_bias, v_t,
      *([cache_k_t] * g_pages), *([cache_v_t] * g_pages), *([cache_lf_t] * g_pages),
      qd_t, knd_t, vnd_t, lfnd_t, scan)


def _mix_tail(d_groups, attn, gates, x, wp_ref, ps_ref, wua_ref, wup_ref, wo_ref):
    pooled = []
    for g in range(len(POOL_WINDOWS)):
        y = jnp.dot(d_groups[g].astype(BF16), wp_ref[g], preferred_element_type=F32)
        pooled.append((y * ps_ref[:, g * POOL_GROUP:(g + 1) * POOL_GROUP]).astype(BF16))
    pool = jnp.concatenate(pooled, axis=1)
    d_model = x.shape[1]
    a = jnp.dot(attn, wua_ref[...], preferred_element_type=F32)
    b = jnp.dot(pool, wup_ref[...], preferred_element_type=F32)
    gates = gates.astype(F32)
    mixed = jax.nn.sigmoid(gates[:, 0:d_model]) * a + jax.nn.sigmoid(gates[:, d_model:2 * d_model]) * b
    return x + jnp.dot(mixed.astype(BF16), wo_ref[...], preferred_element_type=F32)


def _mix_prompt_kernel(attn_ref, u_ref, g_ref, x_ref, wp_ref, ps_ref, wua_ref, wup_ref, wo_ref, o_ref, ext_ref):
    tm = u_ref.shape[0]
    j = pl.program_id(1)

    @pl.when(j == 0)
    def _():
        ext_ref[0:HALO] = jnp.zeros((HALO, POOL_WIDTH), F32)

    @pl.when(j > 0)
    def _():
        ext_ref[0:HALO] = ext_ref[tm:tm + HALO]

    ext_ref[HALO:HALO + tm] = u_ref[...]
    pos = j * tm + lax.broadcasted_iota(jnp.int32, (tm, 1), 0)
    d_groups = []
    for g, w in enumerate(POOL_WINDOWS):
        lanes = slice(g * POOL_GROUP, (g + 1) * POOL_GROUP)
        win = ext_ref[HALO:HALO + tm, lanes]
        for back in range(1, w):
            win = win + ext_ref[HALO - back:HALO - back + tm, lanes]
        cnt = jnp.minimum(pos + 1, w).astype(F32)
        d_groups.append(win / cnt - ext_ref[HALO:HALO + tm, lanes])
    o_ref[...] = _mix_tail(d_groups, attn_ref[...], g_ref[...], x_ref[...], wp_ref, ps_ref, wua_ref, wup_ref, wo_ref)


def _mix_decode_kernel(attn_ref, u_ref, g_ref, x_ref, st_ref, wp_ref, ps_ref, wua_ref, wup_ref, wo_ref, o_ref):
    tm = u_ref.shape[0]
    n_seq = st_ref.shape[1]
    d_groups = []
    for g, w in enumerate(POOL_WINDOWS):
        lanes = slice(g * POOL_GROUP, (g + 1) * POOL_GROUP)
        u_new = u_ref[0:n_seq, lanes]
        win = u_new
        for back in range(1, w):
            win = win + st_ref[POOL_CTX - back, :, lanes]
        d = win / float(w) - u_new
        d_groups.append(jnp.concatenate([d, jnp.zeros((tm - n_seq, POOL_GROUP), F32)], axis=0))
    o_ref[...] = _mix_tail(d_groups, attn_ref[...], g_ref[...], x_ref[...], wp_ref, ps_ref, wua_ref, wup_ref, wo_ref)


def _mix_weight_specs(w, layer, d):
    lay3 = lambda *_: (layer, 0, 0)
    lay4 = lambda *_: (layer, 0, 0, 0)
    specs = [
        _const_spec((None, len(POOL_WINDOWS), POOL_GROUP, POOL_GROUP), lay4),
        _const_spec((None, 1, POOL_WIDTH), lay3),
        _const_spec((None, ATT_WIDTH, d), lay3),
        _const_spec((None, POOL_WIDTH, d), lay3),
        _const_spec((None, d, d), lay3),
    ]
    return specs, (w["w_pool_grp"], w["pool_scale"], w["w_up_attn"], w["w_up_pool"], w["w_out"])


def _mix_prompt(attn, u, g, x, w, layer, tm):
    b, s, d = x.shape
    tok = lambda cols: pl.BlockSpec((None, tm, cols), lambda i, j: (i, j, 0))
    w_specs, w_args = _mix_weight_specs(w, layer, d)
    return pl.pallas_call(
        _mix_prompt_kernel,
        grid=(b, s // tm),
        in_specs=[tok(ATT_WIDTH), tok(POOL_WIDTH), tok(2 * d), tok(d)] + w_specs,
        out_specs=tok(d),
        out_shape=jax.ShapeDtypeStruct((b, s, d), F32),
        scratch_shapes=[pltpu.VMEM((HALO + tm, POOL_WIDTH), F32)],
        compiler_params=pltpu.CompilerParams(dimension_semantics=("parallel", "arbitrary"),
                                             vmem_limit_bytes=VMEM_LIMIT),
        name="mix_prompt",
    )(attn, u, g, x, *w_args)


def _mix_decode(attn, u, g, x, state_t, w, layer):
    rows, d = x.shape
    n_seq = state_t.shape[2]
    full = lambda cols: pl.BlockSpec((rows, cols), lambda i: (0, 0))
    w_specs, w_args = _mix_weight_specs(w, layer, d)
    return pl.pallas_call(
        _mix_decode_kernel,
        grid=(1,),
        in_specs=[full(ATT_WIDTH), full(POOL_WIDTH), full(2 * d), full(d),
                  pl.BlockSpec((None, POOL_CTX, n_seq, POOL_WIDTH), lambda i: (layer, 0, 0, 0))] + w_specs,
        out_specs=full(d),
        out_shape=jax.ShapeDtypeStruct((rows, d), F32),
        compiler_params=pltpu.CompilerParams(dimension_semantics=("arbitrary",), vmem_limit_bytes=VMEM_LIMIT),
        name="mix_decode",
    )(attn, u, g, x, state_t, *w_args)


def _ffn_ple_kernel(x_ref, p_ref, nf_ref, wgate_ref, wup_ref, wdown_ref, np_ref, wpg_ref, wpp_ref, o_ref):
    x = x_ref[...]
    h = _rms_rows(x, nf_ref[...]).astype(BF16)
    hidden = wgate_ref.shape[1]
    chunk = hidden // FFN_CHUNKS
    y = x
    for c in range(FFN_CHUNKS):
        cols = slice(c * chunk, (c + 1) * chunk)
        gate = jnp.dot(h, wgate_ref[:, cols], preferred_element_type=F32)
        up = jnp.dot(h, wup_ref[:, cols], preferred_element_type=F32)
        act = (jax.nn.silu(gate) * up).astype(BF16)
        y = y + jnp.dot(act, wdown_ref[cols, :], preferred_element_type=F32)
    h2 = _rms_rows(y, np_ref[...]).astype(BF16)
    gate2 = jax.nn.sigmoid(jnp.dot(h2, wpg_ref[...], preferred_element_type=F32))
    emb = jnp.dot(p_ref[...].astype(BF16), wpp_ref[...], preferred_element_type=F32)
    o_ref[...] = y + gate2 * emb


def _ffn_ple(x, p_all, w, layer, tm):
    m, d = x.shape
    ple = p_all.shape[2]
    hidden = w["w_down"].shape[1]
    lay = lambda *_: (layer, 0, 0)
    return pl.pallas_call(
        _ffn_ple_kernel,
        grid=(m // tm,),
        in_specs=[
            pl.BlockSpec((tm, d), lambda i: (i, 0)),
            pl.BlockSpec((None, tm, ple), lambda i: (layer, i, 0)),
            _const_spec((None, 1, d), lay),
            _const_spec((None, d, hidden), lay),
            _const_spec((None, d, hidden), lambda *_: (layer, 0, 1)),
            _const_spec((None, hidden, d), lay),
            _const_spec((None, 1, d), lay),
            _const_spec((None, d, d), lay),
            _const_spec((None, ple, d), lay),
        ],
        out_specs=pl.BlockSpec((tm, d), lambda i: (i, 0)),
        out_shape=jax.ShapeDtypeStruct((m, d), F32),
        compiler_params=pltpu.CompilerParams(dimension_semantics=("parallel",), vmem_limit_bytes=VMEM_LIMIT),
        name="ffn_ple",
    )(x, p_all, w["norm_ffn"], w["w_gate_up"], w["w_gate_up"], w["w_down"], w["norm_ple"], w["w_ple_gate"], w["w_ple_proj"])


def _prepare_weights(norm_mix, w_in, b_forget, q_gain, k_gain, w_pool_grp, pool_scale, w_up_attn, w_up_pool,
                     w_out, norm_ffn, w_gate_up, w_down, norm_ple, w_ple_gate, w_ple_proj):
    a = ATT_WIDTH
    f0, u0, g0 = 3 * a, 3 * a + N_HEADS, 3 * a + N_HEADS + POOL_WIDTH
    w_in_t = jnp.swapaxes(w_in, 1, 2)
    wf_t = jnp.pad(w_in_t[:, f0:u0], ((0, 0), (0, 16 - N_HEADS), (0, 0)))
    return {
        "norm_mix": norm_mix[:, None, :],
        "wqkv_t": w_in_t[:, :f0].astype(BF16),
        "wf_t": wf_t.astype(BF16),
        "w_u": w_in[:, :, u0:g0].astype(BF16),
        "w_g": w_in[:, :, g0:].astype(BF16),
        "b_forget": b_forget[:, :, None],
        "q_gain": jnp.tile(q_gain, (1, N_HEADS))[:, :, None],
        "k_gain": jnp.tile(k_gain, (1, N_HEADS))[:, :, None],
        "w_pool_grp": w_pool_grp.astype(BF16),
        "pool_scale": pool_scale[:, None, :],
        "w_up_attn": w_up_attn.astype(BF16),
        "w_up_pool": w_up_pool.astype(BF16),
        "w_out": w_out.astype(BF16),
        "norm_ffn": norm_ffn[:, None, :],
        "w_gate_up": w_gate_up.astype(BF16),
        "w_down": w_down.astype(BF16),
        "norm_ple": norm_ple[:, None, :],
        "w_ple_gate": w_ple_gate.astype(BF16),
        "w_ple_proj": w_ple_proj.astype(BF16),
    }


def _tile(n, preferred):
    return preferred if n % preferred == 0 else n


def kernel(x_prompt, x_sample, cache_k, cache_v, cache_logf, state_pool, page_table, p_prompt, p_sample, norm_mix, w_in, b_forget, q_gain, k_gain, w_pool_grp, pool_scale, w_up_attn, w_up_pool, w_out, norm_ffn, w_gate_up, w_down, norm_ple, w_ple_gate, w_ple_proj):
    depth = w_in.shape[0]
    b, s, d = x_prompt.shape
    n_seq = x_sample.shape[0]
    ple = p_prompt.shape[-1]
    assert x_sample.shape[1] == 1 and n_seq <= DEC_ROWS and s % LANES == 0
    w = _prepare_weights(norm_mix, w_in, b_forget, q_gain, k_gain, w_pool_grp, pool_scale, w_up_attn, w_up_pool,
                         w_out, norm_ffn, w_gate_up, w_down, norm_ple, w_ple_gate, w_ple_proj)

    cache_k_t = jnp.transpose(cache_k, (0, 1, 3, 4, 2))
    cache_v_t = jnp.transpose(cache_v, (0, 1, 3, 4, 2))
    cache_lf_t = jnp.transpose(cache_logf, (0, 1, 3, 2))
    state_t = jnp.transpose(state_pool, (0, 2, 1, 3))

    tm = _tile(s, 512)
    pad_rows = DEC_ROWS - n_seq
    xp = x_prompt
    xs = jnp.pad(x_sample.reshape(n_seq, d), ((0, pad_rows), (0, 0)))
    pp = p_prompt.reshape(depth, b * s, ple)
    ps = jnp.pad(p_sample.reshape(depth, n_seq, ple), ((0, 0), (0, pad_rows), (0, 0)))

    k_p, v_p, lf_p, pool_p, k_s, v_s, lf_s, pool_s = [], [], [], [], [], [], [], []
    for i in range(depth):
        q_t, k_t, v_t, lf_t, u, g, k_rows, k_bias = _inproj(xp, w, i, tm)
        qd_t, kd_t, vd_t, lfd_t, ud, gd, _, _ = _inproj(xs[None], w, i, DEC_ROWS)
        attn, attn_dec_t = _attention(q_t, k_rows, k_bias, v_t, tm, page_table, cache_k_t, cache_v_t, cache_lf_t,
                                      qd_t[0], kd_t[0], vd_t[0], lfd_t[0], i)
        xp = _mix_prompt(attn, u, g, xp, w, i, tm)
        xp = _ffn_ple(xp.reshape(b * s, d), pp, w, i, tm).reshape(b, s, d)
        k_p.append(k_t)
        v_p.append(v_t)
        lf_p.append(lf_t)
        pool_p.append(u[:, s - POOL_CTX:])

        xs = _mix_decode(attn_dec_t.T.astype(BF16), ud[0], gd[0], xs, state_t, w, i)
        xs = _ffn_ple(xs, ps, w, i, DEC_ROWS)
        k_s.append(kd_t[0, :, :n_seq])
        v_s.append(vd_t[0, :, :n_seq])
        lf_s.append(lfd_t[0, :, :n_seq])
        pool_s.append(jnp.concatenate([state_t[i, 1:], ud[0, None, :n_seq]], axis=0))

    def heads_last(xs_t, n_tok):
        x5 = jnp.stack(xs_t).reshape(depth, -1, N_HEADS, HEAD_DIM, n_tok)
        return jnp.transpose(x5, (0, 1, 4, 2, 3))

    k_prompt = heads_last(k_p, s)
    v_prompt = heads_last(v_p, s)
    logf_prompt = jnp.transpose(jnp.stack(lf_p), (0, 1, 3, 2))
    pool_prompt = jnp.stack(pool_p)
    k_sample = jnp.transpose(jnp.stack(k_s), (0, 2, 1)).reshape(depth, n_seq, 1, N_HEADS, HEAD_DIM)
    v_sample = jnp.transpose(jnp.stack(v_s), (0, 2, 1)).reshape(depth, n_seq, 1, N_HEADS, HEAD_DIM)
    logf_sample = jnp.transpose(jnp.stack(lf_s), (0, 2, 1))[:, :, None, :]
    pool_sample = jnp.transpose(jnp.stack(pool_s), (0, 2, 1, 3))
    y_sample = xs[:n_seq].reshape(n_seq, 1, d)
    return (xp, y_sample, k_prompt, v_prompt, logf_prompt, pool_prompt, k_sample, v_sample, logf_sample, pool_sample)
```

```python
import functools
from typing import Any, NamedTuple

import jax
import jax.numpy as jnp
from jax import lax
from jax.experimental import pallas as pl
from jax.experimental.pallas import tpu as pltpu

F32 = jnp.float32
BF16 = jnp.bfloat16

N_HEADS = 8
HEAD_DIM = 64
ATT_WIDTH = N_HEADS * HEAD_DIM
POOL_WINDOWS = (2, 4, 8, 16)
POOL_GROUP = 128
POOL_WIDTH = POOL_GROUP * len(POOL_WINDOWS)
POOL_CTX = max(POOL_WINDOWS) - 1
HALO = POOL_CTX + 1
RMS_EPS = 1e-6
NEG_INF = -1e30
LOG2E = 1.4426950408889634
QK_SCALE_LOG2 = HEAD_DIM ** -0.5 * LOG2E
BIAS_ROWS = 32
PV_EXTRA_ROWS = 16
PV_ROWS = 2 * HEAD_DIM + PV_EXTRA_ROWS
LANES = 128
DEC_ROWS = 128
FFN_CHUNKS = 2
QUERY_BLOCKS = 4
VMEM_LIMIT = 52 * 1024 * 1024

_NT = (((1,), (1,)), ((), ()))


def _const_spec(block, index):
    return pl.BlockSpec(block, index, pipeline_mode=pl.Buffered(1))


def _rms_rows(x, gain):
    return x * lax.rsqrt(jnp.mean(x * x, axis=-1, keepdims=True) + RMS_EPS) * gain


def _log_sigmoid(x):
    return -(jnp.maximum(-x, 0.0) + jnp.log1p(jnp.exp(-jnp.abs(x))))


def _cumsum_lanes(x):
    n = x.shape[-1]
    lane = lax.broadcasted_iota(jnp.int32, x.shape, x.ndim - 1)
    shift = 1
    while shift < n:
        x = x + jnp.where(lane >= shift, pltpu.roll(x, shift, axis=x.ndim - 1), 0.0)
        shift *= 2
    return x


def _split3(x):
    hi = x.astype(BF16).astype(F32)
    mid = (x - hi).astype(BF16).astype(F32)
    lo = (x - hi - mid).astype(BF16).astype(F32)
    return hi, mid, lo


def _inproj_kernel(x_ref, nrm_ref, wqkv_ref, wf_ref, wu_ref, wg_ref, bf_ref, qg_ref, kg_ref,
                   qT_ref, kT_ref, vT_ref, lfT_ref, u_ref, g_ref, kr_ref, kb_ref, va_ref, carry_ref):
    tm = x_ref.shape[0]
    h = _rms_rows(x_ref[...], nrm_ref[...]).astype(BF16)

    def proj_t(rows):
        return lax.dot_general(rows, h, _NT, preferred_element_type=F32)

    def head_norm(y_t, gain_col):
        parts = []
        for hd in range(N_HEADS):
            y = y_t[hd * HEAD_DIM:(hd + 1) * HEAD_DIM]
            parts.append(y * lax.rsqrt(jnp.mean(y * y, axis=0, keepdims=True) + RMS_EPS))
        return jnp.concatenate(parts, axis=0) * gain_col

    q_t = head_norm(proj_t(wqkv_ref[0:ATT_WIDTH, :]), qg_ref[...])
    qT_ref[...] = (q_t * QK_SCALE_LOG2).astype(BF16)
    k_t = head_norm(proj_t(wqkv_ref[ATT_WIDTH:2 * ATT_WIDTH, :]), kg_ref[...])
    kT_ref[...] = k_t
    kr_ref[...] = k_t.T.astype(BF16)
    v_t = proj_t(wqkv_ref[2 * ATT_WIDTH:3 * ATT_WIDTH, :])
    vT_ref[...] = v_t
    ones_rows = jnp.where(lax.broadcasted_iota(jnp.int32, (PV_EXTRA_ROWS, tm), 0) == 0, 1.0, 0.0).astype(BF16)
    v_pairs = []
    for pair in range(N_HEADS // 2):
        v_pairs += [v_t[pair * LANES:(pair + 1) * LANES].astype(BF16), ones_rows]
    va_ref[...] = jnp.concatenate(v_pairs, axis=0)

    logf = _log_sigmoid(proj_t(wf_ref[...])[0:N_HEADS] + bf_ref[...])
    lfT_ref[...] = logf

    @pl.when(pl.program_id(1) == 0)
    def _():
        carry_ref[...] = jnp.zeros_like(carry_ref)

    c = _cumsum_lanes(logf) + carry_ref[:, 0:1]
    carry_ref[...] = jnp.broadcast_to(c[:, tm - 1:tm], carry_ref.shape)

    terms = _split3(c * (-LOG2E))
    row = lax.broadcasted_iota(jnp.int32, (BIAS_ROWS, tm), 0)
    bias_t = jnp.zeros((BIAS_ROWS, tm), F32)
    for hd in range(N_HEADS):
        for term in range(3):
            bias_t = jnp.where(row == 3 * hd + term, terms[term][hd:hd + 1], bias_t)
    bias_t = jnp.concatenate([bias_t, jnp.zeros((LANES - BIAS_ROWS, tm), F32)], axis=0)
    kb_ref[...] = bias_t.T.astype(BF16)

    u_ref[...] = jnp.dot(h, wu_ref[...], preferred_element_type=F32)
    g_ref[...] = jnp.dot(h, wg_ref[...], preferred_element_type=F32).astype(g_ref.dtype)


def _inproj(x, w, layer, tm):
    b, s, d = x.shape
    lay = lambda *_: (layer, 0, 0)
    tok_t = lambda rows: pl.BlockSpec((None, rows, tm), lambda i, j: (i, 0, j))
    tok = lambda cols: pl.BlockSpec((None, tm, cols), lambda i, j: (i, j, 0))
    return pl.pallas_call(
        _inproj_kernel,
        grid=(b, s // tm),
        in_specs=[
            tok(d),
            _const_spec((None, 1, d), lay),
            _const_spec((None, 3 * ATT_WIDTH, d), lay),
            _const_spec((None, 16, d), lay),
            _const_spec((None, d, POOL_WIDTH), lay),
            _const_spec((None, d, 2 * d), lay),
            _const_spec((None, N_HEADS, 1), lay),
            _const_spec((None, ATT_WIDTH, 1), lay),
            _const_spec((None, ATT_WIDTH, 1), lay),
        ],
        out_specs=[tok_t(ATT_WIDTH), tok_t(ATT_WIDTH), tok_t(ATT_WIDTH), tok_t(N_HEADS),
                   tok(POOL_WIDTH), tok(2 * d), tok(ATT_WIDTH), tok(LANES), tok_t(PV_ROWS * N_HEADS // 2)],
        out_shape=[
            jax.ShapeDtypeStruct((b, ATT_WIDTH, s), BF16),
            jax.ShapeDtypeStruct((b, ATT_WIDTH, s), F32),
            jax.ShapeDtypeStruct((b, ATT_WIDTH, s), F32),
            jax.ShapeDtypeStruct((b, N_HEADS, s), F32),
            jax.ShapeDtypeStruct((b, s, POOL_WIDTH), F32),
            jax.ShapeDtypeStruct((b, s, 2 * d), BF16),
            jax.ShapeDtypeStruct((b, s, ATT_WIDTH), BF16),
            jax.ShapeDtypeStruct((b, s, LANES), BF16),
            jax.ShapeDtypeStruct((b, PV_ROWS * N_HEADS // 2, s), BF16),
        ],
        scratch_shapes=[pltpu.VMEM((N_HEADS, LANES), F32)],
        compiler_params=pltpu.CompilerParams(dimension_semantics=("parallel", "arbitrary"),
                                             vmem_limit_bytes=VMEM_LIMIT),
        name="inproj",
    )(x, w["norm_mix"], w["wqkv_t"], w["wf_t"], w["w_u"], w["w_g"], w["b_forget"], w["q_gain"], w["k_gain"])


class _Decode(NamedTuple):
    k: Any
    v: Any
    lf: Any
    q_t: Any
    kn_t: Any
    vn_t: Any
    lfn_t: Any
    scan: Any
    out: Any
    q: Any
    kn: Any
    vn: Any
    lfn: Any
    m: Any
    l: Any
    carry: Any
    acc: Any


def _sum_heads(parts):
    sub = lax.broadcasted_iota(jnp.int32, parts[0].shape, 0)
    x = [parts[h] for h in (0, 4, 2, 6, 1, 5, 3, 7)]
    x = [v + pltpu.roll(v, 4, axis=0) for v in x]
    x = [jnp.where(sub < 4, x[2 * i], x[2 * i + 1]) for i in range(4)]
    x = [v + pltpu.roll(v, 6, axis=0) for v in x]
    x = [jnp.where((sub & 2) == 0, x[2 * i], pltpu.roll(x[2 * i + 1], 2, axis=0)) for i in range(2)]
    x = [v + pltpu.roll(v, 7, axis=0) for v in x]
    return jnp.where((sub & 1) == 0, x[0], pltpu.roll(x[1], 1, axis=0))


def _head_scores(prod):
    parts = []
    for hd in range(N_HEADS):
        part = prod(hd, 0)
        for r in range(1, HEAD_DIM // 8):
            part = part + prod(hd, r)
        parts.append(part)
    return _sum_heads(parts)


def _decode_begin(dec, seq):
    def column(ref):
        x = ref[...].astype(F32)
        lane = lax.broadcasted_iota(jnp.int32, x.shape, 1)
        return jnp.sum(jnp.where(lane == seq, x, 0.0), axis=1, keepdims=True)

    dec.q[...] = jnp.broadcast_to(column(dec.q_t), dec.q.shape)
    dec.kn[...] = column(dec.kn_t)
    dec.vn[...] = column(dec.vn_t)
    dec.lfn[...] = column(dec.lfn_t)
    dec.m[...] = jnp.full_like(dec.m, NEG_INF)
    dec.l[...] = jnp.zeros_like(dec.l)
    dec.carry[...] = jnp.zeros_like(dec.carry)
    dec.acc[...] = jnp.zeros_like(dec.acc)


def _decode_pages(dec):
    g_pages = len(dec.k)
    n_rows = N_HEADS * g_pages
    lf = jnp.concatenate([r[...] for r in dec.lf], axis=0)
    terms = jnp.concatenate([x.astype(BF16) for x in _split3(lf)], axis=0)
    sums = jnp.dot(terms, dec.scan[...], preferred_element_type=F32)
    sums = sums[0:n_rows] + sums[n_rows:2 * n_rows] + sums[2 * n_rows:3 * n_rows]

    carry = dec.carry[...]
    s_pages = []
    for g in range(g_pages):
        c = sums[8 * g:8 * g + 8, 0:LANES] + carry
        carry = carry + sums[8 * g:8 * g + 8, LANES:2 * LANES]
        s = _head_scores(lambda hd, r, g=g: dec.k[g][hd, 8 * r:8 * r + 8] * dec.q[hd * HEAD_DIM + 8 * r:hd * HEAD_DIM + 8 * r + 8])
        s_pages.append(s - c * LOG2E)
    dec.carry[...] = carry

    m_prev = dec.m[...]
    m_new = m_prev
    for s in s_pages:
        m_new = jnp.maximum(m_new, s)
    alpha = jnp.exp2(m_prev - m_new)
    p_pages = [jnp.exp2(s - m_new) for s in s_pages]
    l_new = alpha * dec.l[...]
    for p in p_pages:
        l_new = l_new + p
    dec.l[...] = l_new
    dec.m[...] = m_new
    for hd in range(N_HEADS):
        acc = alpha[hd:hd + 1] * dec.acc[hd]
        for g in range(g_pages):
            acc = acc + p_pages[g][hd:hd + 1] * dec.v[g][hd]
        dec.acc[hd] = acc


def _decode_finish(dec, seq):
    kn = dec.kn[...]
    s_new = _head_scores(lambda hd, r: dec.q[hd * HEAD_DIM + 8 * r:hd * HEAD_DIM + 8 * r + 8]
                         * kn[hd * HEAD_DIM + 8 * r:hd * HEAD_DIM + 8 * r + 8])
    s_new = s_new - (dec.carry[...] + dec.lfn[...]) * LOG2E
    m_lane = dec.m[...]
    m_fin = jnp.maximum(jnp.max(m_lane, axis=1, keepdims=True), s_new)
    w = jnp.exp2(m_lane - m_fin)
    p_new = jnp.exp2(s_new - m_fin)
    l_fin = jnp.sum(dec.l[...] * w, axis=1, keepdims=True) + p_new
    outs = []
    for hd in range(N_HEADS):
        rows = slice(hd * HEAD_DIM, (hd + 1) * HEAD_DIM)
        past = jnp.sum(dec.acc[hd] * w[hd:hd + 1], axis=1, keepdims=True)
        outs.append((past + p_new[hd:hd + 1] * dec.vn[rows]) / l_fin[hd:hd + 1])
    out = jnp.concatenate(outs, axis=0)
    lane = lax.broadcasted_iota(jnp.int32, dec.out.shape, 1)
    dec.out[...] = jnp.where(lane == seq, out, dec.out[...])


def _attn_kernel(b_tab, pair_tab, qi_tab, kj_tab, page_tab, qT_ref, kr_ref, kb_ref, va_ref, *refs,
                 g_pages, groups, n_items):
    n_dec_in = 3 * g_pages + 5
    o_ref, od_ref, qbd_ref, m_ref, acc_ref = refs[n_dec_in:n_dec_in + 5]
    dec = _Decode(refs[0:g_pages], refs[g_pages:2 * g_pages], refs[2 * g_pages:3 * g_pages],
                  *refs[3 * g_pages:n_dec_in], od_ref, *refs[n_dec_in + 5:])
    tq = o_ref.shape[0]
    tk = kr_ref.shape[0]
    item = pl.program_id(0)
    pair = pair_tab[item]
    qi = qi_tab[item]
    kj = kj_tab[item]
    seq = lax.div(item, groups)
    grp = lax.rem(item, groups)
    active = item < n_items

    @pl.when(item == 0)
    def _():
        od_ref[...] = jnp.zeros_like(od_ref)

    @pl.when(jnp.logical_and(active, grp == 0))
    def _():
        _decode_begin(dec, seq)

    @pl.when(kj == 0)
    def _():
        q = qT_ref[...]
        top = lax.broadcasted_iota(jnp.int32, q.shape, 0) < HEAD_DIM
        zero = jnp.zeros_like(q)
        qbd_ref[0:LANES] = jnp.concatenate([jnp.where(top, q, zero), jnp.where(top, zero, q)], axis=1)
        row = lax.broadcasted_iota(jnp.int32, (LANES, 2 * tq), 0) - 6 * pair
        second = lax.broadcasted_iota(jnp.int32, (LANES, 2 * tq), 1) >= tq
        lo = jnp.where(second, 3, 0)
        qbd_ref[LANES:2 * LANES] = jnp.where(jnp.logical_and(row >= lo, row < lo + 3), 1.0, 0.0).astype(BF16)
        m_ref[...] = jnp.full_like(m_ref, NEG_INF)
        acc_ref[...] = jnp.zeros_like(acc_ref)

    def step(on_diagonal):
        _decode_pages(dec)
        keys = jnp.concatenate([kr_ref[...], kb_ref[...]], axis=1)
        s_all = jnp.dot(keys, qbd_ref[...], preferred_element_type=F32)
        width = 2 * tq // QUERY_BLOCKS
        for blk in range(QUERY_BLOCKS):
            cols = slice(blk * width, (blk + 1) * width)
            first_qry = (blk * width) % tq
            n_keys = min(tk, first_qry + width) if on_diagonal else tk
            s = s_all[0:n_keys, cols]
            if on_diagonal:
                key_pos = lax.broadcasted_iota(jnp.int32, (n_keys, width), 0)
                qry_pos = first_qry + lax.broadcasted_iota(jnp.int32, (n_keys, width), 1)
                s = jnp.where(key_pos <= qry_pos, s, NEG_INF)
            m_prev = m_ref[0:1, cols]
            m_new = jnp.maximum(m_prev, jnp.max(s, axis=0, keepdims=True))
            alpha = jnp.exp2(m_prev - m_new)
            p = jnp.exp2(s - m_new).astype(BF16)
            pv = jnp.dot(va_ref[:, 0:n_keys], p, preferred_element_type=F32)
            acc_ref[:, cols] = alpha * acc_ref[:, cols] + pv
            m_ref[:, cols] = jnp.broadcast_to(m_new, (m_ref.shape[0], width))

    @pl.when(kj < qi)
    def _():
        step(False)

    @pl.when(kj == qi)
    def _():
        step(True)
        first = acc_ref[0:HEAD_DIM, 0:tq] / acc_ref[LANES:LANES + 1, 0:tq]
        second = acc_ref[HEAD_DIM:LANES, tq:2 * tq] / acc_ref[LANES:LANES + 1, tq:2 * tq]
        o_ref[...] = jnp.concatenate([first, second], axis=0).T.astype(o_ref.dtype)

    @pl.when(jnp.logical_and(active, grp == groups - 1))
    def _():
        _decode_finish(dec, seq)


def _attention(q_t, k_rows, k_bias, v_pairs, tile, page_table, cache_k_t, cache_v_t, cache_lf_t,
               qd_t, knd_t, vnd_t, lfnd_t, layer):
    b, _, s = q_t.shape
    n = s // tile
    pairs = N_HEADS // 2
    tiles = [(i, j) for i in range(n) for j in range(i + 1)]
    steps = b * pairs * len(tiles)
    n_seq, n_pages = page_table.shape
    g_pages = min(g for g in range(1, n_pages + 1) if n_pages % g == 0 and n_seq * (n_pages // g) <= steps)
    groups = n_pages // g_pages
    n_items = n_seq * groups
    b_tab = jnp.asarray([i for i in range(b) for _ in range(pairs * len(tiles))], jnp.int32)
    pair_tab = jnp.asarray([p for _ in range(b) for p in range(pairs) for _ in tiles], jnp.int32)
    qi_tab = jnp.asarray([t[0] for _ in range(b * pairs) for t in tiles], jnp.int32)
    kj_tab = jnp.asarray([t[1] for _ in range(b * pairs) for t in tiles], jnp.int32)
    item_of_step = jnp.minimum(jnp.arange(steps), n_items - 1)
    page_tab = page_table.reshape(n_items, g_pages)[item_of_step].reshape(-1)

    def kv_page(g):
        return pl.BlockSpec((None, None, N_HEADS, HEAD_DIM, LANES),
                            lambda s, bt, pt, qi, kj, pg: (layer, pg[s * g_pages + g], 0, 0, 0))

    def lf_page(g):
        return pl.BlockSpec((None, None, N_HEADS, LANES), lambda s, bt, pt, qi, kj, pg: (layer, pg[s * g_pages + g], 0, 0))

    whole = lambda rows, cols=LANES: pl.BlockSpec((rows, cols), lambda *a: (0, 0))
    lane_ge_row = lax.broadcasted_iota(jnp.int32, (LANES, LANES), 1) >= lax.broadcasted_iota(jnp.int32, (LANES, LANES), 0)
    scan = jnp.concatenate([lane_ge_row.astype(BF16), jnp.ones((LANES, LANES), BF16)], axis=1)
    grid_spec = pltpu.PrefetchScalarGridSpec(
        num_scalar_prefetch=5,
        grid=(steps,),
        in_specs=([
            pl.BlockSpec((None, LANES, tile), lambda s, bt, pt, qi, kj, pg: (bt[s], pt[s], qi[s])),
            pl.BlockSpec((None, tile, LANES), lambda s, bt, pt, qi, kj, pg: (bt[s], kj[s], pt[s])),
            pl.BlockSpec((None, tile, LANES), lambda s, bt, pt, qi, kj, pg: (bt[s], kj[s], 0)),
            pl.BlockSpec((None, PV_ROWS, tile), lambda s, bt, pt, qi, kj, pg: (bt[s], pt[s], kj[s])),
        ] + [kv_page(g) for g in range(g_pages)] + [kv_page(g) for g in range(g_pages)]
          + [lf_page(g) for g in range(g_pages)]
          + [whole(ATT_WIDTH), whole(ATT_WIDTH), whole(ATT_WIDTH), whole(N_HEADS), whole(LANES, 2 * LANES)]),
        out_specs=[pl.BlockSpec((None, tile, LANES), lambda s, bt, pt, qi, kj, pg: (bt[s], qi[s], pt[s])),
                   whole(ATT_WIDTH)],
        scratch_shapes=[
            pltpu.VMEM((2 * LANES, 2 * tile), BF16),
            pltpu.VMEM((8, 2 * tile), F32),
            pltpu.VMEM((PV_ROWS, 2 * tile), F32),
            pltpu.VMEM((ATT_WIDTH, LANES), F32),
            pltpu.VMEM((ATT_WIDTH, 1), F32),
            pltpu.VMEM((ATT_WIDTH, 1), F32),
            pltpu.VMEM((N_HEADS, 1), F32),
            pltpu.VMEM((N_HEADS, LANES), F32),
            pltpu.VMEM((N_HEADS, LANES), F32),
            pltpu.VMEM((N_HEADS, LANES), F32),
            pltpu.VMEM((N_HEADS, HEAD_DIM, LANES), F32),
        ],
    )
    return pl.pallas_call(
        functools.partial(_attn_kernel, g_pages=g_pages, groups=groups, n_items=n_items),
        grid_spec=grid_spec,
        out_shape=[jax.ShapeDtypeStruct((b, s, ATT_WIDTH), BF16), jax.ShapeDtypeStruct((ATT_WIDTH, LANES), F32)],
        compiler_params=pltpu.CompilerParams(dimension_semantics=("arbitrary",), vmem_limit_bytes=VMEM_LIMIT),
        name="attention",
    )(b_tab, pair_tab, qi_tab, kj_tab, page_tab, q_t, k_rows, k_bias, v_pairs,
      *([cache_k_t] * g_pages), *([cache_v_t] * g_pages), *([cache_lf_t] * g_pages),
      qd_t, knd_t, vnd_t, lfnd_t, scan)


def _mix_tail(d_groups, attn, gates, x, wp_ref, ps_ref, wua_ref, wup_ref, wo_ref):
    pooled = []
    for g in range(len(POOL_WINDOWS)):
        y = jnp.dot(d_groups[g].astype(BF16), wp_ref[g], preferred_element_type=F32)
        pooled.append((y * ps_ref[:, g * POOL_GROUP:(g + 1) * POOL_GROUP]).astype(BF16))
    pool = jnp.concatenate(pooled, axis=1)
    d_model = x.shape[1]
    a = jnp.dot(attn, wua_ref[...], preferred_element_type=F32)
    b = jnp.dot(pool, wup_ref[...], preferred_element_type=F32)
    gates = gates.astype(F32)
    mixed = jax.nn.sigmoid(gates[:, 0:d_model]) * a + jax.nn.sigmoid(gates[:, d_model:2 * d_model]) * b
    return x + jnp.dot(mixed.astype(BF16), wo_ref[...], preferred_element_type=F32)


def _mix_prompt_kernel(attn_ref, u_ref, g_ref, x_ref, wp_ref, ps_ref, wua_ref, wup_ref, wo_ref, o_ref, ext_ref):
    tm = u_ref.shape[0]
    j = pl.program_id(1)

    @pl.when(j == 0)
    def _():
        ext_ref[0:HALO] = jnp.zeros((HALO, POOL_WIDTH), F32)

    @pl.when(j > 0)
    def _():
        ext_ref[0:HALO] = ext_ref[tm:tm + HALO]

    ext_ref[HALO:HALO + tm] = u_ref[...]
    pos = j * tm + lax.broadcasted_iota(jnp.int32, (tm, 1), 0)
    d_groups = []
    for g, w in enumerate(POOL_WINDOWS):
        lanes = slice(g * POOL_GROUP, (g + 1) * POOL_GROUP)
        win = ext_ref[HALO:HALO + tm, lanes]
        for back in range(1, w):
            win = win + ext_ref[HALO - back:HALO - back + tm, lanes]
        cnt = jnp.minimum(pos + 1, w).astype(F32)
        d_groups.append(win / cnt - ext_ref[HALO:HALO + tm, lanes])
    o_ref[...] = _mix_tail(d_groups, attn_ref[...], g_ref[...], x_ref[...], wp_ref, ps_ref, wua_ref, wup_ref, wo_ref)


def _mix_decode_kernel(attn_ref, u_ref, g_ref, x_ref, st_ref, wp_ref, ps_ref, wua_ref, wup_ref, wo_ref, o_ref):
    tm = u_ref.shape[0]
    n_seq = st_ref.shape[1]
    d_groups = []
    for g, w in enumerate(POOL_WINDOWS):
        lanes = slice(g * POOL_GROUP, (g + 1) * POOL_GROUP)
        u_new = u_ref[0:n_seq, lanes]
        win = u_new
        for back in range(1, w):
            win = win + st_ref[POOL_CTX - back, :, lanes]
        d = win / float(w) - u_new
        d_groups.append(jnp.concatenate([d, jnp.zeros((tm - n_seq, POOL_GROUP), F32)], axis=0))
    o_ref[...] = _mix_tail(d_groups, attn_ref[...], g_ref[...], x_ref[...], wp_ref, ps_ref, wua_ref, wup_ref, wo_ref)


def _mix_weight_specs(w, layer, d):
    lay3 = lambda *_: (layer, 0, 0)
    lay4 = lambda *_: (layer, 0, 0, 0)
    specs = [
        _const_spec((None, len(POOL_WINDOWS), POOL_GROUP, POOL_GROUP), lay4),
        _const_spec((None, 1, POOL_WIDTH), lay3),
        _const_spec((None, ATT_WIDTH, d), lay3),
        _const_spec((None, POOL_WIDTH, d), lay3),
        _const_spec((None, d, d), lay3),
    ]
    return specs, (w["w_pool_grp"], w["pool_scale"], w["w_up_attn"], w["w_up_pool"], w["w_out"])


def _mix_prompt(attn, u, g, x, w, layer, tm):
    b, s, d = x.shape
    tok = lambda cols: pl.BlockSpec((None, tm, cols), lambda i, j: (i, j, 0))
    w_specs, w_args = _mix_weight_specs(w, layer, d)
    return pl.pallas_call(
        _mix_prompt_kernel,
        grid=(b, s // tm),
        in_specs=[tok(ATT_WIDTH), tok(POOL_WIDTH), tok(2 * d), tok(d)] + w_specs,
        out_specs=tok(d),
        out_shape=jax.ShapeDtypeStruct((b, s, d), F32),
        scratch_shapes=[pltpu.VMEM((HALO + tm, POOL_WIDTH), F32)],
        compiler_params=pltpu.CompilerParams(dimension_semantics=("parallel", "arbitrary"),
                                             vmem_limit_bytes=VMEM_LIMIT),
        name="mix_prompt",
    )(attn, u, g, x, *w_args)


def _mix_decode(attn, u, g, x, state_t, w, layer):
    rows, d = x.shape
    n_seq = state_t.shape[2]
    full = lambda cols: pl.BlockSpec((rows, cols), lambda i: (0, 0))
    w_specs, w_args = _mix_weight_specs(w, layer, d)
    return pl.pallas_call(
        _mix_decode_kernel,
        grid=(1,),
        in_specs=[full(ATT_WIDTH), full(POOL_WIDTH), full(2 * d), full(d),
                  pl.BlockSpec((None, POOL_CTX, n_seq, POOL_WIDTH), lambda i: (layer, 0, 0, 0))] + w_specs,
        out_specs=full(d),
        out_shape=jax.ShapeDtypeStruct((rows, d), F32),
        compiler_params=pltpu.CompilerParams(dimension_semantics=("arbitrary",), vmem_limit_bytes=VMEM_LIMIT),
        name="mix_decode",
    )(attn, u, g, x, state_t, *w_args)


def _ffn_ple_kernel(x_ref, p_ref, nf_ref, wgate_ref, wup_ref, wdown_ref, np_ref, wpg_ref, wpp_ref, o_ref):
    x = x_ref[...]
    h = _rms_rows(x, nf_ref[...]).astype(BF16)
    hidden = wgate_ref.shape[1]
    chunk = hidden // FFN_CHUNKS
    y = x
    for c in range(FFN_CHUNKS):
        cols = slice(c * chunk, (c + 1) * chunk)
        gate = jnp.dot(h, wgate_ref[:, cols], preferred_element_type=F32)
        up = jnp.dot(h, wup_ref[:, cols], preferred_element_type=F32)
        act = (jax.nn.silu(gate) * up).astype(BF16)
        y = y + jnp.dot(act, wdown_ref[cols, :], preferred_element_type=F32)
    h2 = _rms_rows(y, np_ref[...]).astype(BF16)
    gate2 = jax.nn.sigmoid(jnp.dot(h2, wpg_ref[...], preferred_element_type=F32))
    emb = jnp.dot(p_ref[...].astype(BF16), wpp_ref[...], preferred_element_type=F32)
    o_ref[...] = y + gate2 * emb


def _ffn_ple(x, p_all, w, layer, tm):
    m, d = x.shape
    ple = p_all.shape[2]
    hidden = w["w_down"].shape[1]
    lay = lambda *_: (layer, 0, 0)
    return pl.pallas_call(
        _ffn_ple_kernel,
        grid=(m // tm,),
        in_specs=[
            pl.BlockSpec((tm, d), lambda i: (i, 0)),
            pl.BlockSpec((None, tm, ple), lambda i: (layer, i, 0)),
            _const_spec((None, 1, d), lay),
            _const_spec((None, d, hidden), lay),
            _const_spec((None, d, hidden), lambda *_: (layer, 0, 1)),
            _const_spec((None, hidden, d), lay),
            _const_spec((None, 1, d), lay),
            _const_spec((None, d, d), lay),
            _const_spec((None, ple, d), lay),
        ],
        out_specs=pl.BlockSpec((tm, d), lambda i: (i, 0)),
        out_shape=jax.ShapeDtypeStruct((m, d), F32),
        compiler_params=pltpu.CompilerParams(dimension_semantics=("parallel",), vmem_limit_bytes=VMEM_LIMIT),
        name="ffn_ple",
    )(x, p_all, w["norm_ffn"], w["w_gate_up"], w["w_gate_up"], w["w_down"], w["norm_ple"], w["w_ple_gate"], w["w_ple_proj"])


def _prepare_weights(norm_mix, w_in, b_forget, q_gain, k_gain, w_pool_grp, pool_scale, w_up_attn, w_up_pool,
                     w_out, norm_ffn, w_gate_up, w_down, norm_ple, w_ple_gate, w_ple_proj):
    a = ATT_WIDTH
    f0, u0, g0 = 3 * a, 3 * a + N_HEADS, 3 * a + N_HEADS + POOL_WIDTH
    w_in_t = jnp.swapaxes(w_in, 1, 2)
    wf_t = jnp.pad(w_in_t[:, f0:u0], ((0, 0), (0, 16 - N_HEADS), (0, 0)))
    return {
        "norm_mix": norm_mix[:, None, :],
        "wqkv_t": w_in_t[:, :f0].astype(BF16),
        "wf_t": wf_t.astype(BF16),
        "w_u": w_in[:, :, u0:g0].astype(BF16),
        "w_g": w_in[:, :, g0:].astype(BF16),
        "b_forget": b_forget[:, :, None],
        "q_gain": jnp.tile(q_gain, (1, N_HEADS))[:, :, None],
        "k_gain": jnp.tile(k_gain, (1, N_HEADS))[:, :, None],
        "w_pool_grp": w_pool_grp.astype(BF16),
        "pool_scale": pool_scale[:, None, :],
        "w_up_attn": w_up_attn.astype(BF16),
        "w_up_pool": w_up_pool.astype(BF16),
        "w_out": w_out.astype(BF16),
        "norm_ffn": norm_ffn[:, None, :],
        "w_gate_up": w_gate_up.astype(BF16),
        "w_down": w_down.astype(BF16),
        "norm_ple": norm_ple[:, None, :],
        "w_ple_gate": w_ple_gate.astype(BF16),
        "w_ple_proj": w_ple_proj.astype(BF16),
    }


def _tile(n, preferred):
    return preferred if n % preferred == 0 else n


def kernel(x_prompt, x_sample, cache_k, cache_v, cache_logf, state_pool, page_table, p_prompt, p_sample, norm_mix, w_in, b_forget, q_gain, k_gain, w_pool_grp, pool_scale, w_up_attn, w_up_pool, w_out, norm_ffn, w_gate_up, w_down, norm_ple, w_ple_gate, w_ple_proj):
    depth = w_in.shape[0]
    b, s, d = x_prompt.shape
    n_seq = x_sample.shape[0]
    ple = p_prompt.shape[-1]
    assert x_sample.shape[1] == 1 and n_seq <= DEC_ROWS and s % LANES == 0
    w = _prepare_weights(norm_mix, w_in, b_forget, q_gain, k_gain, w_pool_grp, pool_scale, w_up_attn, w_up_pool,
                         w_out, norm_ffn, w_gate_up, w_down, norm_ple, w_ple_gate, w_ple_proj)

    cache_k_t = jnp.transpose(cache_k, (0, 1, 3, 4, 2))
    cache_v_t = jnp.transpose(cache_v, (0, 1, 3, 4, 2))
    cache_lf_t = jnp.transpose(cache_logf, (0, 1, 3, 2))
    state_t = jnp.transpose(state_pool, (0, 2, 1, 3))

    tm = _tile(s, 512)
    pad_rows = DEC_ROWS - n_seq
    xp = x_prompt
    xs = jnp.pad(x_sample.reshape(n_seq, d), ((0, pad_rows), (0, 0)))
    pp = p_prompt.reshape(depth, b * s, ple)
    ps = jnp.pad(p_sample.reshape(depth, n_seq, ple), ((0, 0), (0, pad_rows), (0, 0)))

    k_p, v_p, lf_p, pool_p, k_s, v_s, lf_s, pool_s = [], [], [], [], [], [], [], []
    for i in range(depth):
        q_t, k_t, v_t, lf_t, u, g, k_rows, k_bias, v_pairs = _inproj(xp, w, i, tm)
        qd_t, kd_t, vd_t, lfd_t, ud, gd, _, _, _ = _inproj(xs[None], w, i, DEC_ROWS)
        attn, attn_dec_t = _attention(q_t, k_rows, k_bias, v_pairs, tm, page_table, cache_k_t, cache_v_t, cache_lf_t,
                                      qd_t[0], kd_t[0], vd_t[0], lfd_t[0], i)
        xp = _mix_prompt(attn, u, g, xp, w, i, tm)
        xp = _ffn_ple(xp.reshape(b * s, d), pp, w, i, tm).reshape(b, s, d)
        k_p.append(k_t)
        v_p.append(v_t)
        lf_p.append(lf_t)
        pool_p.append(u[:, s - POOL_CTX:])

        xs = _mix_decode(attn_dec_t.T.astype(BF16), ud[0], gd[0], xs, state_t, w, i)
        xs = _ffn_ple(xs, ps, w, i, DEC_ROWS)
        k_s.append(kd_t[0, :, :n_seq])
        v_s.append(vd_t[0, :, :n_seq])
        lf_s.append(lfd_t[0, :, :n_seq])
        pool_s.append(jnp.concatenate([state_t[i, 1:], ud[0, None, :n_seq]], axis=0))

    def heads_last(xs_t, n_tok):
        x5 = jnp.stack(xs_t).reshape(depth, -1, N_HEADS, HEAD_DIM, n_tok)
        return jnp.transpose(x5, (0, 1, 4, 2, 3))

    k_prompt = heads_last(k_p, s)
    v_prompt = heads_last(v_p, s)
    logf_prompt = jnp.transpose(jnp.stack(lf_p), (0, 1, 3, 2))
    pool_prompt = jnp.stack(pool_p)
    k_sample = jnp.transpose(jnp.stack(k_s), (0, 2, 1)).reshape(depth, n_seq, 1, N_HEADS, HEAD_DIM)
    v_sample = jnp.transpose(jnp.stack(v_s), (0, 2, 1)).reshape(depth, n_seq, 1, N_HEADS, HEAD_DIM)
    logf_sample = jnp.transpose(jnp.stack(lf_s), (0, 2, 1))[:, :, None, :]
    pool_sample = jnp.transpose(jnp.stack(pool_s), (0, 2, 1, 3))
    y_sample = xs[:n_seq].reshape(n_seq, 1, d)
    return (xp, y_sample, k_prompt, v_prompt, logf_prompt, pool_prompt, k_sample, v_sample, logf_sample, pool_sample)
```
